```python
import jax, jax.numpy as jnp
from jax import lax
import numpy as np

D_MODEL = 4096
BATCH = 4
SEQ = 2048
DEPTH = 4
DEC_BATCH = 8
DEC_SEQ = 4
PAST_LEN = 8192
PAGE_SIZE = 128

N_MIXERS = 3
N_RWKV = (DEPTH + 2) // N_MIXERS
N_RET = (DEPTH + 1) // N_MIXERS
N_FOX = DEPTH // N_MIXERS
N_META = 16
RMS_EPS = 1e-6

RWKV_N = 64
RWKV_H = D_MODEL // RWKV_N
RWKV_LW = 128
RWKV_LA = 128
RWKV_LG = 480
RWKV_GN_EPS = 64e-5

RET_H = 16
RET_DK = D_MODEL // RET_H
RET_DV = 2 * RET_DK
RET_CHUNK = 128
ROPE_BASE = 10000.0

FOX_H = 32
FOX_HD = D_MODEL // FOX_H
FOX_BLOCK = 128

D_FF = 11008
CONV_W = 3

N_PAGES = PAST_LEN // PAGE_SIZE
N_USED = DEC_BATCH * N_PAGES
N_POOL = N_USED + (N_USED + 3) // 4

F32 = jnp.float32

kernel_name = 'hybrid_rwkv7_retnet_fox_convffn_step'


def rms_norm(x, g):
    xf = x.astype(F32)
    y = xf * lax.rsqrt(jnp.mean(xf * xf, axis=-1, keepdims=True) + RMS_EPS)
    return (y * g.astype(F32)).astype(x.dtype)


def rwkv7_mix(x, wkv0, shift0, mu, w0, w1, w2, a0, a1, a2, g1, g2, k_k, k_a, r_k, wr, wk, wv, wo, lnx_g, lnx_b):
    bsz, t, _ = x.shape
    x_prev = jnp.concatenate([shift0[:, None, :].astype(x.dtype), x[:, :-1]], axis=1)
    xx = x_prev - x
    xr, xw, xk, xv, xa, xg = (x + xx * mu[i] for i in range(6))
    r = xr @ wr
    k = xk @ wk
    v = xv @ wv
    w_log = -jax.nn.softplus(-(w0 + jnp.tanh(xw @ w1) @ w2)) - 0.5
    a = jax.nn.sigmoid(a0 + (xa @ a1) @ a2)
    g = jax.nn.sigmoid(xg @ g1) @ g2
    heads = lambda z: z.reshape(bsz, t, RWKV_H, RWKV_N)
    kk = heads(k * k_k).astype(F32)
    kk = kk * lax.rsqrt(jnp.sum(kk * kk, axis=-1, keepdims=True) + 1e-12)
    k = k * (1.0 + (a - 1.0) * k_a)
    r_h, k_h, v_h, a_h = heads(r), heads(k), heads(v), heads(a)
    decay = jnp.exp(-jnp.exp(heads(w_log).astype(F32)))

    def step(S, inp):
        r_t, d_t, k_t, v_t, kk_t, a_t = inp
        sa = jnp.einsum('bhvk,bhk->bhv', S, -kk_t)
        S = S * d_t[:, :, None, :] + sa[..., None] * (kk_t * a_t)[:, :, None, :] + v_t[..., None] * k_t[:, :, None, :]
        return S, jnp.einsum('bhvk,bhk->bhv', S, r_t)

    seqs = tuple(jnp.moveaxis(z.astype(F32), 1, 0) for z in (r_h, decay, k_h, v_h, kk, a_h))
    S_fin, o = lax.scan(step, wkv0.astype(F32), seqs)
    o = jnp.moveaxis(o, 0, 1)
    mean = jnp.mean(o, axis=-1, keepdims=True)
    var = jnp.mean(jnp.square(o - mean), axis=-1, keepdims=True)
    o = ((o - mean) * lax.rsqrt(var + RWKV_GN_EPS)).reshape(bsz, t, D_MODEL) * lnx_g + lnx_b
    bonus = (jnp.sum(r_h * k_h * r_k, axis=-1, keepdims=True) * v_h).reshape(bsz, t, D_MODEL)
    out = ((o + bonus) * g).astype(x.dtype) @ wo
    return out, S_fin.astype(x.dtype), x[:, -1]


def rotary(z, pos):
    half = z.shape[-1] // 2
    inv = ROPE_BASE ** (-jnp.arange(half, dtype=F32) / half)
    ang = pos.astype(F32)[:, None] * inv[None, :]
    cos = jnp.cos(ang)[None, :, None, :]
    sin = jnp.sin(ang)[None, :, None, :]
    z1 = z[..., :half].astype(F32)
    z2 = z[..., half:].astype(F32)
    return jnp.concatenate([z1 * cos - z2 * sin, z1 * sin + z2 * cos], axis=-1)


def ret_log_gamma():
    return jnp.log1p(-jnp.exp2(-5.0 - jnp.arange(RET_H, dtype=F32)))


def ret_chunk(S, q, k, v, lg):
    L = q.shape[1]
    idx = jnp.arange(L, dtype=F32)
    diff = idx[:, None] - idx[None, :]
    causal = (diff >= 0)[None]
    dmat = jnp.where(causal, jnp.exp(jnp.where(causal, diff[None], 0.0) * lg[:, None, None]), 0.0)
    scores = jnp.einsum('bihd,bjhd->bhij', q, k) * dmat[None]
    o = jnp.einsum('bhij,bjhe->bihe', scores, v)
    inner = jnp.exp((idx[:, None] + 1.0) * lg[None, :])
    o = o + jnp.einsum('bihd,bhde->bihe', q, S) * inner[None, :, :, None]
    kdec = jnp.exp((L - 1.0 - idx)[:, None] * lg[None, :])
    S = jnp.exp(L * lg)[None, :, None, None] * S + jnp.einsum('bjhd,bjhe->bhde', k * kdec[None, :, :, None], v)
    return S, o


def retention_mix(x, S0, pos0, n_lead, wq, wk, wv, wg, wo):
    bsz, t, _ = x.shape
    pos = pos0 + jnp.arange(t)
    q = rotary((x @ wq).reshape(bsz, t, RET_H, RET_DK), pos)
    k = rotary((x @ wk).reshape(bsz, t, RET_H, RET_DK), pos) * RET_DK ** -0.5
    v = (x @ wv).reshape(bsz, t, RET_H, RET_DV).astype(F32)
    lg = ret_log_gamma()
    S, o_lead = ret_chunk(S0.astype(F32), q[:, :n_lead], k[:, :n_lead], v[:, :n_lead], lg)
    outs = [o_lead]
    n_rest = t - n_lead
    if n_rest > 0:
        nc = n_rest // RET_CHUNK
        to_chunks = lambda z: jnp.moveaxis(z[:, n_lead:].reshape(bsz, nc, RET_CHUNK, z.shape[2], z.shape[3]), 1, 0)
        S, o_rest = lax.scan(lambda s, qkv: ret_chunk(s, qkv[0], qkv[1], qkv[2], lg), S,
                             (to_chunks(q), to_chunks(k), to_chunks(v)))
        outs.append(jnp.moveaxis(o_rest, 0, 1).reshape(bsz, n_rest, RET_H, RET_DV))
    o = jnp.concatenate(outs, axis=1)
    o = o * lax.rsqrt(jnp.mean(o * o, axis=-1, keepdims=True) + RMS_EPS)
    o = o.reshape(bsz, t, RET_H * RET_DV)
    out = (jax.nn.silu(x @ wg).astype(F32) * o).astype(x.dtype) @ wo
    return out, S.astype(x.dtype)


def fox_project(x, wq, wk, wv, wf, bf):
    bsz, t, _ = x.shape
    sh = lambda z: z.reshape(bsz, t, FOX_H, FOX_HD)
    logf = jax.nn.log_sigmoid((x @ wf + bf).astype(F32))
    return sh(x @ wq), sh(x @ wk), sh(x @ wv), logf


def fox_attend(q, cq, qpos, k, v, ck, kpos):
    s = jnp.einsum('bqhd,bkhd->bhqk', q, k).astype(F32) * FOX_HD ** -0.5
    s = s + jnp.moveaxis(cq, 1, 2)[..., :, None] - jnp.moveaxis(ck, 1, 2)[..., None, :]
    s = jnp.where((kpos[None, :] <= qpos[:, None])[None, None], s, -jnp.inf)
    p = jax.nn.softmax(s, axis=-1)
    return jnp.einsum('bhqk,bkhd->bqhd', p.astype(v.dtype), v)


def fox_prompt(x, wq, wk, wv, wf, bf, wo):
    bsz, t, _ = x.shape
    q, k, v, logf = fox_project(x, wq, wk, wv, wf, bf)
    c = jnp.cumsum(logf, axis=1)
    pos = jnp.arange(t)
    o_meta = fox_attend(q[:, :N_META], c[:, :N_META], pos[:N_META], k[:, :N_META], v[:, :N_META],
                        c[:, :N_META], pos[:N_META])
    nb = (t - N_META) // FOX_BLOCK

    def block(b):
        start = N_META + b * FOX_BLOCK
        qb = lax.dynamic_slice_in_dim(q, start, FOX_BLOCK, axis=1)
        cb = lax.dynamic_slice_in_dim(c, start, FOX_BLOCK, axis=1)
        return fox_attend(qb, cb, start + jnp.arange(FOX_BLOCK), k, v, c, pos)

    o_real = lax.map(block, jnp.arange(nb))
    o_real = jnp.moveaxis(o_real, 0, 1).reshape(bsz, nb * FOX_BLOCK, FOX_H, FOX_HD)
    o = jnp.concatenate([o_meta, o_real], axis=1).reshape(bsz, t, D_MODEL)
    return o @ wo, k, v, logf.astype(x.dtype)


def fox_sample(x, cache_k, cache_v, cache_logf, page_table, wq, wk, wv, wf, bf, wo):
    bsz, t, _ = x.shape
    q, k, v, logf = fox_project(x, wq, wk, wv, wf, bf)
    past_k = cache_k[page_table].reshape(bsz, PAST_LEN, FOX_H, FOX_HD).astype(k.dtype)
    past_v = cache_v[page_table].reshape(bsz, PAST_LEN, FOX_H, FOX_HD).astype(v.dtype)
    past_f = cache_logf[page_table].reshape(bsz, PAST_LEN, FOX_H).astype(F32)
    k_all = jnp.concatenate([past_k, k], axis=1)
    v_all = jnp.concatenate([past_v, v], axis=1)
    c_all = jnp.cumsum(jnp.concatenate([past_f, logf], axis=1), axis=1)
    o = fox_attend(q, c_all[:, PAST_LEN:], PAST_LEN + jnp.arange(t), k_all, v_all, c_all,
                   jnp.arange(PAST_LEN + t))
    return o.reshape(bsz, t, D_MODEL) @ wo, k, v, logf.astype(x.dtype)


def conv_ffn(x, conv_state, w_gate, w_up, conv_w, conv_b, w_down):
    t = x.shape[1]
    u = jnp.concatenate([conv_state.astype(x.dtype), x @ w_gate], axis=1)
    c = conv_b + sum(conv_w[i] * u[:, i:i + t] for i in range(CONV_W))
    h = jax.nn.gelu(c, approximate=False) * (x @ w_up)
    return h @ w_down, u[:, t:]


def setup_inputs(seed: int = 0) -> dict:
    key = jax.random.key(seed)
    ks = iter(jax.random.split(key, 64))
    nrm = lambda shape, s=1.0: jax.random.normal(next(ks), shape, F32) * s
    unif = lambda shape, lo, hi: jax.random.uniform(next(ks), shape, F32, lo, hi)
    D = D_MODEL
    page_table = jax.random.permutation(next(ks), N_POOL)[:N_USED].reshape(DEC_BATCH, N_PAGES).astype(jnp.int32)
    return {
        'x_prompt': nrm((BATCH, SEQ, D)),
        'x_sample': nrm((DEC_BATCH, DEC_SEQ, D)),
        'state_rwkv_wkv': nrm((N_RWKV, DEC_BATCH, RWKV_H, RWKV_N, RWKV_N), 0.1),
        'state_rwkv_shift': nrm((N_RWKV, DEC_BATCH, D)),
        'state_ret': nrm((N_RET, DEC_BATCH, RET_H, RET_DK, RET_DV), 0.5),
        'cache_fox_k': nrm((N_FOX, N_POOL, PAGE_SIZE, FOX_H, FOX_HD)),
        'cache_fox_v': nrm((N_FOX, N_POOL, PAGE_SIZE, FOX_H, FOX_HD)),
        'cache_fox_logf': jax.nn.log_sigmoid(nrm((N_FOX, N_POOL, PAGE_SIZE, FOX_H)) + 2.5),
        'state_ffn_conv': nrm((DEPTH, DEC_BATCH, CONV_W - 1, D_FF)),
        'page_table': page_table,
        'meta_tokens': nrm((N_META, D)),
        'norm_mix_pre': 1.0 + nrm((DEPTH, D), 0.05),
        'norm_mix_post': 1.0 + nrm((DEPTH, D), 0.05),
        'norm_ffn_pre': 1.0 + nrm((DEPTH, D), 0.05),
        'norm_ffn_post': 1.0 + nrm((DEPTH, D), 0.05),
        'rwkv_mu': unif((N_RWKV, 6, D), 0.0, 1.0),
        'rwkv_w0': unif((N_RWKV, D), -6.0, 1.0),
        'rwkv_w1': nrm((N_RWKV, D, RWKV_LW), D ** -0.5),
        'rwkv_w2': nrm((N_RWKV, RWKV_LW, D), 0.1 * RWKV_LW ** -0.5),
        'rwkv_a0': nrm((N_RWKV, D), 0.5),
        'rwkv_a1': nrm((N_RWKV, D, RWKV_LA), D ** -0.5),
        'rwkv_a2': nrm((N_RWKV, RWKV_LA, D), RWKV_LA ** -0.5),
        'rwkv_g1': nrm((N_RWKV, D, RWKV_LG), D ** -0.5),
        'rwkv_g2': nrm((N_RWKV, RWKV_LG, D), RWKV_LG ** -0.5),
        'rwkv_k_k': 0.85 + nrm((N_RWKV, D), 0.05),
        'rwkv_k_a': 1.0 + nrm((N_RWKV, D), 0.05),
        'rwkv_r_k': nrm((N_RWKV, RWKV_H, RWKV_N), 0.1),
        'rwkv_wr': nrm((N_RWKV, D, D), D ** -0.5),
        'rwkv_wk': nrm((N_RWKV, D, D), D ** -0.5),
        'rwkv_wv': nrm((N_RWKV, D, D), D ** -0.5),
        'rwkv_wo': nrm((N_RWKV, D, D), D ** -0.5),
        'rwkv_lnx_g': 1.0 + nrm((N_RWKV, D), 0.05),
        'rwkv_lnx_b': nrm((N_RWKV, D), 0.01),
        'ret_wq': nrm((N_RET, D, RET_H * RET_DK), D ** -0.5),
        'ret_wk': nrm((N_RET, D, RET_H * RET_DK), D ** -0.5),
        'ret_wv': nrm((N_RET, D, RET_H * RET_DV), D ** -0.5),
        'ret_wg': nrm((N_RET, D, RET_H * RET_DV), D ** -0.5),
        'ret_wo': nrm((N_RET, RET_H * RET_DV, D), (RET_H * RET_DV) ** -0.5),
        'fox_wq': nrm((N_FOX, D, D), D ** -0.5),
        'fox_wk': nrm((N_FOX, D, D), D ** -0.5),
        'fox_wv': nrm((N_FOX, D, D), D ** -0.5),
        'fox_wf': nrm((N_FOX, D, FOX_H), D ** -0.5),
        'fox_bf': unif((N_FOX, FOX_H), 1.0, 4.0),
        'fox_wo': nrm((N_FOX, D, D), D ** -0.5),
        'ffn_wgate': nrm((DEPTH, D, D_FF), D ** -0.5),
        'ffn_wup': nrm((DEPTH, D, D_FF), D ** -0.5),
        'ffn_conv_w': nrm((DEPTH, CONV_W, D_FF), CONV_W ** -0.5),
        'ffn_conv_b': nrm((DEPTH, D_FF), 0.01),
        'ffn_wdown': nrm((DEPTH, D_FF, D), D_FF ** -0.5),
    }


def reference(x_prompt, x_sample, state_rwkv_wkv, state_rwkv_shift, state_ret, cache_fox_k, cache_fox_v,
              cache_fox_logf, state_ffn_conv, page_table, meta_tokens, norm_mix_pre, norm_mix_post,
              norm_ffn_pre, norm_ffn_post, rwkv_mu, rwkv_w0, rwkv_w1, rwkv_w2, rwkv_a0, rwkv_a1, rwkv_a2,
              rwkv_g1, rwkv_g2, rwkv_k_k, rwkv_k_a, rwkv_r_k, rwkv_wr, rwkv_wk, rwkv_wv, rwkv_wo,
              rwkv_lnx_g, rwkv_lnx_b, ret_wq, ret_wk, ret_wv, ret_wg, ret_wo, fox_wq, fox_wk, fox_wv,
              fox_wf, fox_bf, fox_wo, ffn_wgate, ffn_wup, ffn_conv_w, ffn_conv_b, ffn_wdown):
    bsz = x_prompt.shape[0]
    meta = jnp.broadcast_to(meta_tokens[None].astype(x_prompt.dtype), (bsz, N_META, D_MODEL))
    xp = jnp.concatenate([meta, x_prompt], axis=1)
    xs = x_sample
    p_wkv, p_shift, s_wkv, s_shift = [], [], [], []
    p_ret, s_ret = [], []
    p_fk, p_fv, p_ff, s_fk, s_fv, s_ff = [], [], [], [], [], []
    p_conv, s_conv = [], []
    for i in range(DEPTH):
        j = i // N_MIXERS
        kind = i % N_MIXERS
        hp = rms_norm(xp, norm_mix_pre[i])
        hs = rms_norm(xs, norm_mix_pre[i])
        if kind == 0:
            rw = (rwkv_mu[j], rwkv_w0[j], rwkv_w1[j], rwkv_w2[j], rwkv_a0[j], rwkv_a1[j], rwkv_a2[j],
                  rwkv_g1[j], rwkv_g2[j], rwkv_k_k[j], rwkv_k_a[j], rwkv_r_k[j], rwkv_wr[j], rwkv_wk[j],
                  rwkv_wv[j], rwkv_wo[j], rwkv_lnx_g[j], rwkv_lnx_b[j])
            op, st_p, sh_p = rwkv7_mix(hp, jnp.zeros((bsz, RWKV_H, RWKV_N, RWKV_N), hp.dtype),
                                       jnp.zeros((bsz, D_MODEL), hp.dtype), *rw)
            os_, st_s, sh_s = rwkv7_mix(hs, state_rwkv_wkv[j], state_rwkv_shift[j], *rw)
            p_wkv.append(st_p); p_shift.append(sh_p); s_wkv.append(st_s); s_shift.append(sh_s)
        elif kind == 1:
            rt = (ret_wq[j], ret_wk[j], ret_wv[j], ret_wg[j], ret_wo[j])
            op, rs_p = retention_mix(hp, jnp.zeros((bsz, RET_H, RET_DK, RET_DV), hp.dtype), 0, N_META, *rt)
            os_, rs_s = retention_mix(hs, state_ret[j], PAST_LEN, hs.shape[1], *rt)
            p_ret.append(rs_p); s_ret.append(rs_s)
        else:
            fw = (fox_wq[j], fox_wk[j], fox_wv[j], fox_wf[j], fox_bf[j], fox_wo[j])
            op, kp, vp, fp_ = fox_prompt(hp, *fw)
            os_, ks_, vs_, fs_ = fox_sample(hs, cache_fox_k[j], cache_fox_v[j], cache_fox_logf[j], page_table, *fw)
            p_fk.append(kp); p_fv.append(vp); p_ff.append(fp_)
            s_fk.append(ks_); s_fv.append(vs_); s_ff.append(fs_)
        xp = xp + rms_norm(op, norm_mix_post[i])
        xs = xs + rms_norm(os_, norm_mix_post[i])
        hp = rms_norm(xp, norm_ffn_pre[i])
        hs = rms_norm(xs, norm_ffn_pre[i])
        cw = (ffn_wgate[i], ffn_wup[i], ffn_conv_w[i], ffn_conv_b[i], ffn_wdown[i])
        fp, cp = conv_ffn(hp, jnp.zeros((bsz, CONV_W - 1, D_FF), hp.dtype), *cw)
        fs, cs = conv_ffn(hs, state_ffn_conv[i], *cw)
        p_conv.append(cp); s_conv.append(cs)
        xp = xp + rms_norm(fp, norm_ffn_post[i])
        xs = xs + rms_norm(fs, norm_ffn_post[i])
    y_prompt = xp[:, N_META:]
    y_sample = xs
    return (y_prompt, y_sample,
            jnp.stack(p_wkv), jnp.stack(p_shift), jnp.stack(p_ret),
            jnp.stack(p_fk), jnp.stack(p_fv), jnp.stack(p_ff), jnp.stack(p_conv),
            jnp.stack(s_wkv), jnp.stack(s_shift), jnp.stack(s_ret),
            jnp.stack(s_fk), jnp.stack(s_fv), jnp.stack(s_ff), jnp.stack(s_conv))
```

```python
import functools

import jax
import jax.numpy as jnp
from jax import lax
from jax.experimental import pallas as pl
from jax.experimental.pallas import tpu as pltpu

F32 = jnp.float32
BF16 = jnp.bfloat16
HIGHEST = lax.Precision.HIGHEST

D = 4096
NB = 4
SEQ = 2048
N_META = 16
TP = SEQ + N_META
DEC_B = 8
TS = 4
TT = 2080
SEQS = ((0, TP), (TP, TS), (TP + TS, TS))
M = NB * TT
DEPTH = 4
RMS_EPS = 1e-6

RWKV_N = 64
RWKV_H = D // RWKV_N
RWKV_GN_EPS = 64e-5
RWKV_L = 64
RWKV_LG = 480
RWKV_LG_PAD = 512

RET_H = 16
RET_DK = D // RET_H
RET_DV = 2 * RET_DK
RET_L = 128
ROPE_BASE = 10000.0
PAST_LEN = 8192

FOX_H = 32
FOX_HD = D // FOX_H
FOX_BQ = 512
PAGE = 128
N_PAGES = PAST_LEN // PAGE

D_FF = 11008

LANES = 128
V7X_VMEM_BYTES = 64 * 1024 * 1024
V7X_VMEM_CAP = 56 * 1024 * 1024

NT_DIMS = (((1,), (1,)), ((), ()))
TN_DIMS = (((0,), (0,)), ((), ()))


def _params(sems, vmem_bytes):
    return pltpu.CompilerParams(
        dimension_semantics=sems,
        vmem_limit_bytes=int(min(max(vmem_bytes * 5 // 4, 16 * 1024 * 1024), V7X_VMEM_CAP)),
    )


def _rms(x, g):
    return x * lax.rsqrt(jnp.mean(x * x, axis=-1, keepdims=True) + RMS_EPS) * g


ROW_TILE = 208


def _mix_body(x_ref, g_ref, mu_ref, sr_ref, o0, o1, o2, o3, o4, o5, hl_ref, carry, *, tr, nchunks):
    c = pl.program_id(1)

    @pl.when(c == 0)
    def _():
        carry[...] = jnp.zeros_like(carry)

    h = _rms(x_ref[...], g_ref[...])
    row = lax.broadcasted_iota(jnp.int32, (tr, 1), 0) + c * tr
    prev = pltpu.roll(h, 1, 0)
    prev = jnp.where(row == c * tr, carry[7:8, :], prev)
    for s, (start, _) in enumerate(SEQS):
        prev = jnp.where(row == start, sr_ref[0, s:s + 1, :], prev)
    carry[...] = h[tr - 8:, :]
    xx = prev - h
    for i, o in enumerate((o0, o1, o2, o3, o4, o5)):
        o[...] = (h + xx * mu_ref[i:i + 1, :]).astype(BF16)

    @pl.when(c == nchunks - 1)
    def _():
        hl_ref[0] = h[tr - 24:tr - 8, :]


def _rwkv_mix(x, g3, layer, mu, j, shift_rows):
    tr = ROW_TILE
    nchunks = TT // tr
    row_spec = pl.BlockSpec((tr, D), lambda b, c: (b * nchunks + c, 0))
    outs = pl.pallas_call(
        functools.partial(_mix_body, tr=tr, nchunks=nchunks),
        grid=(NB, nchunks),
        in_specs=[
            row_spec,
            pl.BlockSpec((None, 1, D), lambda b, c: (layer, 0, 0)),
            pl.BlockSpec((None, 6, D), lambda b, c: (j, 0, 0)),
            pl.BlockSpec((1, 8, D), lambda b, c: (b, 0, 0)),
        ],
        out_specs=[row_spec] * 6 + [pl.BlockSpec((1, 16, D), lambda b, c: (b, 0, 0))],
        out_shape=[jax.ShapeDtypeStruct((M, D), BF16)] * 6 + [jax.ShapeDtypeStruct((NB, 16, D), F32)],
        scratch_shapes=[pltpu.VMEM((8, D), F32)],
        compiler_params=_params(("arbitrary", "arbitrary"), 48 * 1024 * 1024),
        name="rwkv_mix",
    )(x, g3, mu, shift_rows)
    return outs[:6], outs[6]


def _resnorm_body(res_ref, y_ref, gp_ref, gn_ref, xo_ref, ho_ref):
    x = res_ref[...] + _rms(y_ref[...], gp_ref[...])
    xo_ref[...] = x
    ho_ref[...] = _rms(x, gn_ref[...]).astype(BF16)


def _res_body(res_ref, y_ref, gp_ref, xo_ref):
    xo_ref[...] = res_ref[...] + _rms(y_ref[...], gp_ref[...])


def _resnorm(res, y, gpost3, lp, gnext3=None, ln=0):
    tr = ROW_TILE
    row_spec = pl.BlockSpec((tr, D), lambda i: (i, 0))
    gspec = lambda l: pl.BlockSpec((None, 1, D), lambda i: (l, 0, 0))
    if gnext3 is None:
        return pl.pallas_call(
            _res_body, grid=(M // tr,), in_specs=[row_spec, row_spec, gspec(lp)], out_specs=row_spec,
            out_shape=jax.ShapeDtypeStruct((M, D), F32),
            compiler_params=_params(("arbitrary",), 32 * 1024 * 1024), name="res_add",
        )(res, y, gpost3), None
    return pl.pallas_call(
        _resnorm_body, grid=(M // tr,), in_specs=[row_spec, row_spec, gspec(lp), gspec(ln)],
        out_specs=[row_spec, row_spec],
        out_shape=[jax.ShapeDtypeStruct((M, D), F32), jax.ShapeDtypeStruct((M, D), BF16)],
        compiler_params=_params(("arbitrary",), 40 * 1024 * 1024), name="res_norm",
    )(res, y, gpost3, gnext3)


def _mm_body(x_ref, w_ref, o_ref):
    o_ref[...] = jnp.dot(x_ref[...], w_ref[...].astype(BF16), preferred_element_type=F32).astype(o_ref.dtype)


def _mm_acc_body(x_ref, w_ref, a_ref, o_ref):
    o_ref[...] = a_ref[...] + jnp.dot(x_ref[...], w_ref[...].astype(BF16), preferred_element_type=F32)


def _matmul(x, w, layer, *, tm, tn, out_dtype=F32, name, kblock=None, acc=None):
    m, k = x.shape
    n = w.shape[-1]
    kidx, ksz = (0, k) if kblock is None else kblock
    osz = jnp.dtype(out_dtype).itemsize
    vmem = tm * ksz * 2 + 2 * ksz * tn * 4 + ksz * tn * 2 + 2 * tm * tn * osz + tm * tn * 4 + (4 << 20)
    in_specs = [
        pl.BlockSpec((tm, ksz), lambda i, j: (i, kidx), pipeline_mode=pl.Buffered(1)),
        pl.BlockSpec((None, ksz, tn), lambda i, j: (layer, kidx, j)),
    ]
    out_spec = pl.BlockSpec((tm, tn), lambda i, j: (i, j))
    args = (x, w)
    body = _mm_body
    aliases = {}
    if acc is not None:
        in_specs.append(out_spec)
        args = (x, w, acc)
        body = _mm_acc_body
        aliases = {2: 0}
        vmem += 2 * tm * tn * 4
    return pl.pallas_call(
        body,
        grid=(m // tm, n // tn),
        in_specs=in_specs,
        out_specs=out_spec,
        out_shape=jax.ShapeDtypeStruct((m, n), out_dtype),
        input_output_aliases=aliases,
        compiler_params=_params(("arbitrary", "arbitrary"), vmem),
        name=name,
    )(*args)


def _lora_body(x_ref, w1_ref, w2_ref, b_ref, o_ref, *, kind):
    t = jnp.dot(x_ref[...], w1_ref[...].astype(BF16), preferred_element_type=F32)
    if kind == "decay":
        t = jnp.tanh(t)
    elif kind == "gate":
        t = jax.nn.sigmoid(t)
    z = jnp.dot(t.astype(BF16), w2_ref[...].astype(BF16), preferred_element_type=F32)
    if kind == "decay":
        z = -jnp.exp(-jax.nn.softplus(-(b_ref[...] + z)) - 0.5)
    elif kind == "iclr":
        z = jax.nn.sigmoid(b_ref[...] + z)
    o_ref[...] = z.astype(o_ref.dtype)


def _lora(x, w1, w2, bias3, j, *, kind, out_dtype, tm=416):
    m, k = x.shape
    r = w1.shape[-1]
    n = w2.shape[-1]
    vmem = 2 * tm * k * 2 + 3 * (k * r + r * n) * 4 + 3 * tm * n * 4 + (4 << 20)
    return pl.pallas_call(
        functools.partial(_lora_body, kind=kind),
        grid=(m // tm,),
        in_specs=[
            pl.BlockSpec((tm, k), lambda i: (i, 0)),
            pl.BlockSpec((None, k, r), lambda i: (j, 0, 0), pipeline_mode=pl.Buffered(1)),
            pl.BlockSpec((None, r, n), lambda i: (j, 0, 0), pipeline_mode=pl.Buffered(1)),
            pl.BlockSpec((None, 1, n), lambda i: (j, 0, 0)),
        ],
        out_specs=pl.BlockSpec((tm, n), lambda i: (i, 0)),
        out_shape=jax.ShapeDtypeStruct((m, n), out_dtype),
        compiler_params=_params(("arbitrary",), vmem),
        name="rwkv_lora_" + kind,
    )(x, w1, w2, bias3)


def _ffn_up_body(x_ref, wg_ref, wu_ref, cw_ref, cb_ref, st_ref, h_ref, so_ref):
    x = x_ref[...]
    g = jnp.dot(x, wg_ref[...].astype(BF16), preferred_element_type=F32)
    u = jnp.dot(x, wu_ref[...].astype(BF16), preferred_element_type=F32)
    row = lax.broadcasted_iota(jnp.int32, (TT, 1), 0)
    s1 = pltpu.roll(g, 1, 0)
    s2 = pltpu.roll(g, 2, 0)
    for s, (start, length) in enumerate(SEQS):
        st = st_ref[0, s]
        s1 = jnp.where(row == start, st[1:2, :], s1)
        s2 = jnp.where(row == start, st[0:1, :], jnp.where(row == start + 1, st[1:2, :], s2))
        so_ref[0, s] = g[start + length - 2:start + length, :]
    cw = cw_ref[...]
    c = cb_ref[...] + cw[0:1, :] * s2 + cw[1:2, :] * s1 + cw[2:3, :] * g
    act = 0.5 * c * (1.0 + lax.erf(c * (2.0 ** -0.5)))
    h_ref[...] = (act * u).astype(BF16)


def _ffn_up(h, wg, wu, cw, cb3, state, layer, *, tn=256):
    nj = D_FF // tn
    vmem = TT * D * 2 + 4 * D * tn * 4 + 2 * D * tn * 2 + 2 * TT * tn * 2 + 6 * TT * tn * 4 + (4 << 20)
    return pl.pallas_call(
        _ffn_up_body,
        grid=(NB, nj),
        in_specs=[
            pl.BlockSpec((TT, D), lambda b, j: (b, 0), pipeline_mode=pl.Buffered(1)),
            pl.BlockSpec((None, D, tn), lambda b, j: (layer, 0, j)),
            pl.BlockSpec((None, D, tn), lambda b, j: (layer, 0, j)),
            pl.BlockSpec((None, 3, tn), lambda b, j: (layer, 0, j)),
            pl.BlockSpec((None, 1, tn), lambda b, j: (layer, 0, j)),
            pl.BlockSpec((1, 3, 2, tn), lambda b, j: (b, 0, 0, j)),
        ],
        out_specs=[
            pl.BlockSpec((TT, tn), lambda b, j: (b, j)),
            pl.BlockSpec((1, 3, 2, tn), lambda b, j: (b, 0, 0, j)),
        ],
        out_shape=[jax.ShapeDtypeStruct((M, D_FF), BF16), jax.ShapeDtypeStruct((NB, 3, 2, D_FF), F32)],
        compiler_params=_params(("arbitrary", "arbitrary"), vmem),
        name="ffn_up",
    )(h, wg, wu, cw, cb3, state)


def _split3(a, b):
    ah = a.astype(BF16)
    al = (a - ah.astype(F32)).astype(BF16)
    bh = b.astype(BF16)
    bl = (b - bh.astype(F32)).astype(BF16)
    d = lambda p, q: jnp.dot(p, q, preferred_element_type=F32)
    return d(ah, bh) + (d(ah, bl) + d(al, bh))


def _rwkv_consts():
    n = 2 * RWKV_L
    lane = lax.broadcasted_iota(jnp.int32, (1, LANES), 1)
    ri = lax.broadcasted_iota(jnp.int32, (n, n), 0)
    ci = lax.broadcasted_iota(jnp.int32, (n, n), 1)
    same = (ri // RWKV_L) == (ci // RWKV_L)
    li = lax.broadcasted_iota(jnp.int32, (RWKV_L, RWKV_L), 0)
    lj = lax.broadcasted_iota(jnp.int32, (RWKV_L, RWKV_L), 1)
    return dict(
        m0=lane < RWKV_N,
        strict=jnp.logical_and(same, ci < ri),
        incl=jnp.logical_and(same, ci <= ri),
        eye=(ri == ci).astype(F32),
        tri=(lj <= li).astype(F32),
    )


def _rwkv_chunk(r, k, v, ld, a, w, kk_c, ka_c, cst, active=None):
    L = RWKV_L
    m0 = cst["m0"]

    def seg(z):
        s0 = jnp.sum(jnp.where(m0, z, 0.0), axis=-1, keepdims=True)
        s1 = jnp.sum(jnp.where(m0, 0.0, z), axis=-1, keepdims=True)
        return jnp.where(m0, s0, s1)

    kraw = k * kk_c
    kk = kraw * lax.rsqrt(seg(kraw * kraw) + 1e-12)
    kmod = k * (1.0 + (a - 1.0) * ka_c)
    aa = -kk
    bb = kk * a
    km = kmod
    vv = v
    if active is not None:
        ld, aa, bb, km, vv = (jnp.where(active, z, 0.0) for z in (ld, aa, bb, km, vv))
    cum = jnp.dot(cst["tri"], ld, precision=HIGHEST, preferred_element_type=F32)
    cum_l = cum[L - 1:L, :]
    e_neg = jnp.exp(-cum)
    e_rem = jnp.exp(cum_l - cum)

    def stack(z):
        return jnp.concatenate([jnp.where(m0, z, 0.0), jnp.where(m0, 0.0, z)], axis=0)

    rt = stack(r * jnp.exp(cum))
    a2 = stack(aa * jnp.exp(cum - ld)).astype(BF16)
    r2 = rt.astype(BF16)
    b2 = stack(bb * e_neg).astype(BF16)
    k2 = stack(km * e_neg).astype(BF16)
    bh2 = stack(bb * e_rem).astype(BF16)
    kh2 = stack(km * e_rem).astype(BF16)
    v2 = stack(vv).astype(BF16)

    dot = lambda p, q: jnp.dot(p, q, preferred_element_type=F32)
    xa = jnp.concatenate([a2, r2], axis=0)
    xb = lax.dot_general(xa, b2, NT_DIMS, preferred_element_type=F32)
    xk = lax.dot_general(xa, k2, NT_DIMS, preferred_element_type=F32)
    n = 2 * L
    a_ab = jnp.where(cst["strict"], xb[:n], 0.0)
    a_rb = jnp.where(cst["incl"], xb[n:], 0.0)
    a_ak = jnp.where(cst["strict"], xk[:n], 0.0)
    a_rk = jnp.where(cst["incl"], xk[n:], 0.0)
    tinv = cst["eye"] + a_ab
    q = a_ab
    for _ in range(max(L.bit_length() - 2, 0)):
        qb = q.astype(BF16)
        q = dot(qb, qb)
        tinv = tinv + dot(tinv.astype(BF16), q.astype(BF16))
    akv = dot(a_ak.astype(BF16), v2)
    tu = dot(tinv.astype(BF16), jnp.concatenate([akv.astype(BF16), a2], axis=1)).astype(BF16)
    ru = dot(a_rb.astype(BF16), tu)
    o0 = ru[:, :LANES] + dot(a_rk.astype(BF16), v2)
    om = rt + ru[:, LANES:]
    bu = lax.dot_general(bh2, tu, TN_DIMS, preferred_element_type=F32)
    psi = bu[:, :LANES] + lax.dot_general(kh2, v2, TN_DIMS, preferred_element_type=F32)
    phi = cst["eye"] * jnp.exp(cum_l) + bu[:, LANES:]
    o2 = dot(om.astype(BF16), w.astype(BF16)) + o0
    w_new = _split3(phi, w) + psi
    return o2[:L] + o2[L:], w_new, kmod


def _rwkv_body(r_ref, k_ref, v_ref, g_ref, ld_ref, a_ref, kk_ref, ka_ref, rk_ref, lg_ref, lb_ref, s0_ref,
               y_ref, so_ref, *, npairs):
    L = RWKV_L
    nfull = TP // L
    cst = _rwkv_consts()
    m0 = cst["m0"]

    def seg(z):
        s0 = jnp.sum(jnp.where(m0, z, 0.0), axis=-1, keepdims=True)
        s1 = jnp.sum(jnp.where(m0, 0.0, z), axis=-1, keepdims=True)
        return jnp.where(m0, s0, s1)

    def do_chunk(p, rows, w, active):
        ln = slice(LANES * p, LANES * (p + 1))
        r = r_ref[rows, ln].astype(F32)
        k = k_ref[rows, ln].astype(F32)
        v = v_ref[rows, ln].astype(F32)
        o, w_new, kmod = _rwkv_chunk(r, k, v, ld_ref[rows, ln], a_ref[rows, ln], w, kk_ref[:, ln], ka_ref[:, ln],
                                     cst, active)
        mean = seg(o) * (1.0 / RWKV_N)
        xc = o - mean
        var = seg(xc * xc) * (1.0 / RWKV_N)
        on = xc * lax.rsqrt(var + RWKV_GN_EPS) * lg_ref[:, ln] + lb_ref[:, ln]
        bonus = seg(r * kmod * rk_ref[:, ln]) * v
        return (on + bonus) * g_ref[rows, ln].astype(F32), w_new

    def body(c, ws):
        rows = pl.ds(pl.multiple_of(c * L, L), L)
        out = []
        for p in range(npairs):
            y, w_new = do_chunk(p, rows, ws[p], None)
            y_ref[rows, LANES * p:LANES * (p + 1)] = y.astype(BF16)
            out.append(w_new)
        return tuple(out)

    ws = lax.fori_loop(0, nfull, body, tuple(jnp.zeros((LANES, LANES), F32) for _ in range(npairs)))

    win = slice(TT - L, TT)
    lrow = lax.broadcasted_iota(jnp.int32, (L, 1), 0) + (TT - L)
    zero = jnp.zeros((RWKV_N, RWKV_N), F32)
    for p in range(npairs):
        ytail = jnp.zeros((L, LANES), F32)
        for s, (start, length) in enumerate(SEQS):
            lo = max(start, nfull * L)
            active = jnp.logical_and(lrow >= lo, lrow < start + length)
            if s == 0:
                w0 = ws[p]
            else:
                s_in = s0_ref[0, s - 1, 2 * p:2 * p + 2]
                st = jnp.concatenate([jnp.concatenate([s_in[0], zero], axis=1),
                                      jnp.concatenate([zero, s_in[1]], axis=1)], axis=0)
                w0 = st.T
            y, w_new = do_chunk(p, win, w0, active)
            ytail = jnp.where(active, y, ytail)
            wt = w_new.T
            so_ref[0, s, 2 * p] = wt[:RWKV_N, :RWKV_N]
            so_ref[0, s, 2 * p + 1] = wt[RWKV_N:, RWKV_N:]
        y_ref[nfull * L:TT, LANES * p:LANES * (p + 1)] = ytail[nfull * L - (TT - L):].astype(BF16)


def _rwkv(r, k, v, g, ld, a, kk3, ka3, rk3, lg3, lb3, s0, j, *, npairs=2):
    pw = LANES * npairs
    nh = 2 * npairs
    act = lambda: pl.BlockSpec((TT, pw), lambda b, p: (b, p))
    par = lambda: pl.BlockSpec((None, 1, pw), lambda b, p: (j, 0, p))
    vmem = 2 * TT * pw * (4 * 2 + 2 * 4 + 2) + (12 << 20)
    return pl.pallas_call(
        functools.partial(_rwkv_body, npairs=npairs),
        grid=(NB, RWKV_H // nh),
        in_specs=[act(), act(), act(), act(), act(), act(), par(), par(), par(), par(), par(),
                  pl.BlockSpec((None, 1, 2, nh, RWKV_N, RWKV_N), lambda b, p: (j, b, 0, p, 0, 0))],
        out_specs=[act(), pl.BlockSpec((1, 3, nh, RWKV_N, RWKV_N), lambda b, p: (b, 0, p, 0, 0))],
        out_shape=[jax.ShapeDtypeStruct((M, D), BF16),
                   jax.ShapeDtypeStruct((NB, 3, RWKV_H, RWKV_N, RWKV_N), F32)],
        compiler_params=_params(("arbitrary", "arbitrary"), vmem),
        name="rwkv7",
    )(r, k, v, g, ld, a, kk3, ka3, rk3, lg3, lb3, s0)


def _ret_body(q_ref, k_ref, v_ref, g_ref, cos_ref, sin_ref, lg_ref, s0_ref, y_ref, so_ref):
    L = RET_L
    nfull = TP // L
    half = RET_DK // 2
    lg = lg_ref[:, :1]
    ii = lax.broadcasted_iota(jnp.int32, (L, L), 0)
    jj = lax.broadcasted_iota(jnp.int32, (L, L), 1)
    ri = lax.broadcasted_iota(jnp.int32, (L, 1), 0)
    dfull = jnp.where(jj <= ii, jnp.exp((ii - jj).astype(F32) * lg), 0.0)

    def rot(z, cs, sn):
        z1 = z[:, :half]
        z2 = z[:, half:]
        return jnp.concatenate([z1 * cs - z2 * sn, z1 * sn + z2 * cs], axis=1)

    def chunk(rows, s_in, off, length):
        cs = cos_ref[rows, :]
        sn = sin_ref[rows, :]
        q = rot(q_ref[rows, :], cs, sn).astype(BF16)
        k = rot(k_ref[rows, :], cs, sn) * (RET_DK ** -0.5)
        v = v_ref[rows, :]
        nloc = (ri - off).astype(F32)
        act = jnp.logical_and(ri >= off, ri < off + length)
        if off == 0 and length == L:
            dmat = dfull
        else:
            dmat = jnp.where(jnp.logical_and(jj >= off, ii < off + length), dfull, 0.0)
        inner = jnp.exp((nloc + 1.0) * lg)
        kdec = jnp.where(act, jnp.exp((length - 1.0 - nloc) * lg), 0.0)
        sc = lax.dot_general(q, k.astype(BF16), NT_DIMS, preferred_element_type=F32) * dmat
        o = jnp.dot(sc.astype(BF16), v, preferred_element_type=F32)
        o = o + jnp.dot(q, s_in.astype(BF16), preferred_element_type=F32) * inner
        s_out = jnp.exp(length * lg) * s_in + lax.dot_general((k * kdec).astype(BF16), v, TN_DIMS,
                                                                preferred_element_type=F32)
        o = o * lax.rsqrt(jnp.mean(o * o, axis=-1, keepdims=True) + RMS_EPS)
        gate = g_ref[rows, :].astype(F32)
        return gate * jax.nn.sigmoid(gate) * o, s_out, act

    so_ref[0, 0, 0] = jnp.zeros((RET_DK, RET_DV), F32)

    def body(c, carry):
        rows = pl.ds(pl.multiple_of(c * L, L), L)
        y, s_out, _ = chunk(rows, so_ref[0, 0, 0], 0, L)
        y_ref[rows, :] = y.astype(BF16)
        so_ref[0, 0, 0] = s_out
        return carry

    lax.fori_loop(0, nfull, body, 0)

    win = slice(TT - L, TT)
    ytail = jnp.zeros((L, RET_DV), F32)
    for s, (start, length) in enumerate(SEQS):
        lo = max(start, nfull * L)
        s_in = so_ref[0, 0, 0] if s == 0 else s0_ref[0, s - 1, 0]
        y, s_out, act = chunk(win, s_in, lo - (TT - L), start + length - lo)
        ytail = jnp.where(act, y, ytail)
        so_ref[0, s, 0] = s_out
    y_ref[nfull * L:TT, :] = ytail[nfull * L - (TT - L):].astype(BF16)


def _retention(q, k, v, g, cos, sin, lg3, s0):
    spec = lambda w: pl.BlockSpec((TT, w), lambda b, h: (b, h))
    tab = pl.BlockSpec((TT, RET_DK // 2), lambda b, h: (0, 0))
    vmem = 2 * TT * (2 * RET_DK * 4 + 2 * RET_DV * 2 + RET_DV * 2 + 2 * 128 * 4) + 12 * RET_DK * RET_DV * 4 + (8 << 20)
    return pl.pallas_call(
        _ret_body,
        grid=(NB, RET_H),
        in_specs=[spec(RET_DK), spec(RET_DK), spec(RET_DV), spec(RET_DV), tab, tab,
                  pl.BlockSpec((None, 1, LANES), lambda b, h: (h, 0, 0)),
                  pl.BlockSpec((1, 2, 1, RET_DK, RET_DV), lambda b, h: (b, 0, h, 0, 0))],
        out_specs=[spec(RET_DV), pl.BlockSpec((1, 3, 1, RET_DK, RET_DV), lambda b, h: (b, 0, h, 0, 0))],
        out_shape=[jax.ShapeDtypeStruct((M, RET_H * RET_DV), BF16),
                   jax.ShapeDtypeStruct((NB, 3, RET_H, RET_DK, RET_DV), F32)],
        compiler_params=_params(("arbitrary", "arbitrary"), vmem),
        name="retention",
    )(q, k, v, g, cos, sin, lg3, s0)


def _fproj_body(x_ref, w_ref, b_ref, o_ref):
    z = jnp.dot(x_ref[...], w_ref[...].astype(BF16), preferred_element_type=F32) + b_ref[...]
    o_ref[...] = jax.nn.log_sigmoid(z)


def _fox_logf(h, wf_pad, bf_pad, *, tm=416):
    return pl.pallas_call(
        _fproj_body, grid=(M // tm,),
        in_specs=[pl.BlockSpec((tm, D), lambda i: (i, 0)), pl.BlockSpec((D, LANES), lambda i: (0, 0)),
                  pl.BlockSpec((1, LANES), lambda i: (0, 0))],
        out_specs=pl.BlockSpec((tm, LANES), lambda i: (i, 0)),
        out_shape=jax.ShapeDtypeStruct((M, LANES), F32),
        compiler_params=_params(("arbitrary",), 24 * 1024 * 1024), name="fox_logf",
    )(h, wf_pad, bf_pad)


def _tri(n):
    return (lax.broadcasted_iota(jnp.int32, (n, n), 1) <= lax.broadcasted_iota(jnp.int32, (n, n), 0)).astype(F32)


def _pcum_body(f_ref, c_ref):
    n = LANES
    tri = _tri(n)
    carry = jnp.zeros((1, LANES), F32)
    nfull = TP // n
    for blk in range(nfull):
        cb = jnp.dot(tri, f_ref[0, blk * n:(blk + 1) * n, :], precision=HIGHEST, preferred_element_type=F32) + carry
        c_ref[0, blk * n:(blk + 1) * n, :] = cb
        carry = cb[n - 1:n, :]
    row = lax.broadcasted_iota(jnp.int32, (n, 1), 0) + (TT - n)
    x = jnp.where(row >= nfull * n, f_ref[0, TT - n:TT, :], 0.0)
    cb = jnp.dot(tri, x, precision=HIGHEST, preferred_element_type=F32) + carry
    c_ref[0, nfull * n:TT, :] = cb[nfull * n - (TT - n):, :]


def _fox_prompt_cum(f3):
    spec = pl.BlockSpec((1, TT, LANES), lambda b: (b, 0, 0))
    return pl.pallas_call(
        _pcum_body, grid=(NB,), in_specs=[spec], out_specs=spec,
        out_shape=jax.ShapeDtypeStruct((NB, TT, LANES), F32),
        compiler_params=_params(("arbitrary",), 16 * 1024 * 1024), name="fox_cum",
    )(f3)


def _softmax_step(s, vb, m, l, acc):
    m_new = jnp.maximum(m, jnp.max(s, axis=-1, keepdims=True))
    alpha = jnp.exp(m - m_new)
    p = jnp.exp(s - m_new)
    l = alpha * l + jnp.sum(p, axis=-1, keepdims=True)
    acc = alpha * acc + jnp.dot(p.astype(BF16), vb, preferred_element_type=F32)
    return m_new, l, acc


def _fox_body(q_ref, k_ref, v_ref, c_ref, ct_ref, dec_ref, y_ref):
    h = pl.program_id(1)
    bq = FOX_BQ
    nq = TP // bq
    scale = FOX_HD ** -0.5
    pick = (lax.broadcasted_iota(jnp.int32, (LANES, LANES), 0) == h).astype(F32)

    def attend(qrows, nrows, kv_blocks):
        q = (q_ref[qrows, :] * scale).astype(BF16)
        cq = jnp.dot(c_ref[0, qrows, :], pick, precision=HIGHEST, preferred_element_type=F32)[:, :1]
        m = jnp.full((nrows, 1), -jnp.inf, F32)
        l = jnp.zeros((nrows, 1), F32)
        acc = jnp.zeros((nrows, FOX_HD), F32)
        for krows, mask in kv_blocks:
            s = lax.dot_general(q, k_ref[krows, :].astype(BF16), NT_DIMS, preferred_element_type=F32)
            s = s + cq - ct_ref[:, krows]
            if mask is not None:
                s = jnp.where(mask, s, -jnp.inf)
            m, l, acc = _softmax_step(s, v_ref[krows, :].astype(BF16), m, l, acc)
        return acc / l

    rr = lax.broadcasted_iota(jnp.int32, (bq, bq), 0)
    cc = lax.broadcasted_iota(jnp.int32, (bq, bq), 1)
    causal = cc <= rr
    blocks = [slice(i * bq, (i + 1) * bq) for i in range(nq)]
    for qi in range(nq):
        kv = [(blocks[kj], causal if kj == qi else None) for kj in range(qi + 1)]
        y_ref[blocks[qi], :] = attend(blocks[qi], bq, kv).astype(BF16)
    nt = TT - nq * bq
    qrow = lax.broadcasted_iota(jnp.int32, (nt, LANES), 0) + nq * bq
    kcol = lax.broadcasted_iota(jnp.int32, (nt, LANES), 1) + (TT - LANES)
    tail_mask = jnp.logical_and(kcol >= nq * bq, kcol <= qrow)
    kv = [(b, None) for b in blocks] + [(slice(TT - LANES, TT), tail_mask)]
    o = attend(slice(nq * bq, TT), nt, kv)
    npr = TP - nq * bq
    y_ref[nq * bq:TT, :] = jnp.concatenate(
        [o[:npr], dec_ref[0], jnp.zeros((TT - TP - 2 * TS, FOX_HD), F32)], axis=0).astype(BF16)


def _fox_prompt(q, k, v, c3, ct3, dec):
    spec = pl.BlockSpec((TT, FOX_HD), lambda b, h: (b, h))
    return pl.pallas_call(
        _fox_body,
        grid=(NB, FOX_H),
        in_specs=[spec, spec, spec,
                  pl.BlockSpec((1, TT, LANES), lambda b, h: (b, 0, 0)),
                  pl.BlockSpec((None, None, 1, TT), lambda b, h: (b, h, 0, 0)),
                  pl.BlockSpec((1, 2 * TS, FOX_HD), lambda b, h: (b, 0, h))],
        out_specs=spec,
        out_shape=jax.ShapeDtypeStruct((M, D), BF16),
        compiler_params=_params(("arbitrary", "arbitrary"), 40 * 1024 * 1024),
        name="fox_prompt",
    )(q, k, v, c3, ct3, dec)


def _dcum_body(pt_ref, cf_ref, nf_ref, o_ref, carry):
    del pt_ref
    p = pl.program_id(1)

    @pl.when(p == 0)
    def _():
        carry[...] = jnp.zeros_like(carry)

    x = jnp.where(p == N_PAGES, nf_ref[0], cf_ref[0])
    cb = jnp.dot(_tri(PAGE), x, precision=HIGHEST, preferred_element_type=F32) + carry[...]
    o_ref[0, 0] = cb
    carry[...] = cb[PAGE - 1:PAGE, :]


def _fox_decode_cum(page_table, cache_f, new_f):
    grid_spec = pltpu.PrefetchScalarGridSpec(
        num_scalar_prefetch=1,
        grid=(DEC_B, N_PAGES + 1),
        in_specs=[
            pl.BlockSpec((1, PAGE, FOX_H), lambda b, p, pt: (pt[b, jnp.minimum(p, N_PAGES - 1)], 0, 0)),
            pl.BlockSpec((1, PAGE, FOX_H), lambda b, p, pt: (b, 0, 0)),
        ],
        out_specs=pl.BlockSpec((1, 1, PAGE, FOX_H), lambda b, p, pt: (b, p, 0, 0)),
        scratch_shapes=[pltpu.VMEM((1, FOX_H), F32)],
    )
    return pl.pallas_call(
        _dcum_body, grid_spec=grid_spec,
        out_shape=jax.ShapeDtypeStruct((DEC_B, N_PAGES + 1, PAGE, FOX_H), F32),
        compiler_params=_params(("arbitrary", "arbitrary"), 16 * 1024 * 1024), name="fox_decode_cum",
    )(page_table, cache_f, new_f)


def _dattn_body(pt_ref, q_ref, ck_ref, cv_ref, crow_ref, cq_ref, kn_ref, vn_ref, cn_ref, o_ref, m_s, l_s, acc_s):
    del pt_ref
    p = pl.program_id(1)
    nrow = TS * FOX_H

    @pl.when(p == 0)
    def _():
        m_s[...] = jnp.full_like(m_s, -jnp.inf)
        l_s[...] = jnp.zeros_like(l_s)
        acc_s[...] = jnp.zeros_like(acc_s)

    q = (q_ref[0] * (FOX_HD ** -0.5)).astype(BF16)
    cq = cq_ref[0][:, :1]
    hrow = lax.broadcasted_iota(jnp.int32, (nrow, 1), 0) % FOX_H

    def update(s, vb):
        m, l, acc = _softmax_step(s, vb, m_s[...], l_s[...], acc_s[...])
        m_s[...] = m
        l_s[...] = l
        acc_s[...] = acc

    @pl.when(p < N_PAGES)
    def _():
        kb = ck_ref[0].reshape(PAGE * FOX_H, FOX_HD).astype(BF16)
        vb = cv_ref[0].reshape(PAGE * FOX_H, FOX_HD).astype(BF16)
        s = lax.dot_general(q, kb, NT_DIMS, preferred_element_type=F32) + cq - crow_ref[0, 0]
        hcol = lax.broadcasted_iota(jnp.int32, (1, PAGE * FOX_H), 1) % FOX_H
        update(jnp.where(hcol == hrow, s, -jnp.inf), vb)

    @pl.when(p == N_PAGES)
    def _():
        s = lax.dot_general(q, kn_ref[0].astype(BF16), NT_DIMS, preferred_element_type=F32) + cq - cn_ref[0]
        col = lax.broadcasted_iota(jnp.int32, (1, nrow), 1)
        row = lax.broadcasted_iota(jnp.int32, (nrow, 1), 0)
        valid = jnp.logical_and(col % FOX_H == hrow, col // FOX_H <= row // FOX_H)
        update(jnp.where(valid, s, -jnp.inf), vn_ref[0].astype(BF16))
        o_ref[0] = acc_s[...] / l_s[...]


def _fox_decode_attn(page_table, q2, cache_k, cache_v, crow, cq, kn, vn, cn):
    nrow = TS * FOX_H
    page = lambda b, p, pt: (pt[b, jnp.minimum(p, N_PAGES - 1)], 0, 0, 0)
    per_b = pl.BlockSpec((1, nrow, FOX_HD), lambda b, p, pt: (b, 0, 0))
    grid_spec = pltpu.PrefetchScalarGridSpec(
        num_scalar_prefetch=1,
        grid=(DEC_B, N_PAGES + 1),
        in_specs=[
            per_b,
            pl.BlockSpec((1, PAGE, FOX_H, FOX_HD), page),
            pl.BlockSpec((1, PAGE, FOX_H, FOX_HD), page),
            pl.BlockSpec((1, 1, 1, PAGE * FOX_H), lambda b, p, pt: (b, jnp.minimum(p, N_PAGES - 1), 0, 0)),
            per_b, per_b, per_b,
            pl.BlockSpec((1, 1, nrow), lambda b, p, pt: (b, 0, 0)),
        ],
        out_specs=per_b,
        scratch_shapes=[pltpu.VMEM((nrow, 1), F32), pltpu.VMEM((nrow, 1), F32), pltpu.VMEM((nrow, FOX_HD), F32)],
    )
    return pl.pallas_call(
        _dattn_body, grid_spec=grid_spec,
        out_shape=jax.ShapeDtypeStruct((DEC_B, nrow, FOX_HD), F32),
        compiler_params=_params(("arbitrary", "arbitrary"), 40 * 1024 * 1024), name="fox_decode_attn",
    )(page_table, q2, cache_k, cache_v, crow, cq, kn, vn, cn)


def _sample_rows(a):
    n = a.shape[-1]
    return a.reshape(NB, TT, n)[:, TP:TP + 2 * TS].reshape(DEC_B, TS, n)


def _prompt_rows(a):
    n = a.shape[-1]
    return a.reshape(NB, TT, n)[:, :TP]


def _pack_states(prompt_like_zero, sample):
    s = sample.reshape((NB, 2) + sample.shape[1:])
    return jnp.concatenate([jnp.zeros_like(s[:, :1]) if prompt_like_zero else s[:, :0], s], axis=1)


def _rwkv_layer(x, layer, j, gpre3, state_wkv6, state_shift, mu, w0, w1, w2, a0, a1, a2, g1p, g2p, k_k, k_a, r_k,
                wr, wk, wv, wo, lnx_g, lnx_b):
    shift_rows = jnp.concatenate(
        [jnp.zeros((NB, 1, D), F32), state_shift[j].reshape(NB, 2, D), jnp.zeros((NB, 5, D), F32)], axis=1)
    (xr, xw, xk, xv, xa, xg), hl = _rwkv_mix(x, gpre3, layer, mu, j, shift_rows)
    mm = functools.partial(_matmul, tm=TT, tn=512, out_dtype=BF16)
    r = mm(xr, wr, j, name="rwkv_r")
    k = mm(xk, wk, j, name="rwkv_k")
    v = mm(xv, wv, j, name="rwkv_v")
    row3 = lambda p: p.reshape(p.shape[0], 1, D)
    ld = _lora(xw, w1, w2, row3(w0), j, kind="decay", out_dtype=F32)
    a = _lora(xa, a1, a2, row3(a0), j, kind="iclr", out_dtype=F32)
    g = _lora(xg, g1p, g2p, row3(a0), j, kind="gate", out_dtype=BF16)
    y, states = _rwkv(r, k, v, g, ld, a, row3(k_k), row3(k_a), r_k.reshape(-1, 1, D), row3(lnx_g), row3(lnx_b),
                      state_wkv6, j)
    out = _matmul(y, wo, j, tm=TT, tn=512, name="rwkv_o")
    p_wkv = states[:, 0]
    s_wkv = states[:, 1:].reshape(DEC_B, RWKV_H, RWKV_N, RWKV_N)
    p_shift = hl[:, 7]
    s_shift = jnp.stack([hl[:, 11], hl[:, 15]], axis=1).reshape(DEC_B, D)
    return out, (p_wkv, p_shift, s_wkv, s_shift)


def _ret_tables():
    half = RET_DK // 2
    inv = ROPE_BASE ** (-jnp.arange(half, dtype=F32) / half)
    row = jnp.arange(TT)
    pos = jnp.where(row < TP, row, PAST_LEN + jnp.maximum(row - TP, 0) % TS)
    ang = pos.astype(F32)[:, None] * inv[None, :]
    lg = jnp.log1p(-jnp.exp2(-5.0 - jnp.arange(RET_H, dtype=F32)))
    return jnp.cos(ang), jnp.sin(ang), jnp.broadcast_to(lg[:, None, None], (RET_H, 1, LANES))


def _ret_layer(h, state_ret, wq, wk, wv, wg, wo):
    q = _matmul(h, wq, 0, tm=TT, tn=512, name="ret_q")
    k = _matmul(h, wk, 0, tm=TT, tn=512, name="ret_k")
    v = _matmul(h, wv, 0, tm=TT, tn=512, out_dtype=BF16, name="ret_v")
    g = _matmul(h, wg, 0, tm=TT, tn=512, out_dtype=BF16, name="ret_g")
    cos, sin, lg3 = _ret_tables()
    s0 = state_ret[0].reshape(NB, 2, RET_H, RET_DK, RET_DV)
    y, states = _retention(q, k, v, g, cos, sin, lg3, s0)
    out = _matmul(y, wo, 0, tm=TT // 2, tn=256, name="ret_o")
    return out, (states[:, 0], states[:, 1:].reshape(DEC_B, RET_H, RET_DK, RET_DV))


def _fox_layer(h, cache_k, cache_v, cache_f, page_table, wq, wk, wv, wf, bf, wo):
    q = _matmul(h, wq, 0, tm=TT, tn=512, name="fox_q")
    k = _matmul(h, wk, 0, tm=TT, tn=512, name="fox_k")
    v = _matmul(h, wv, 0, tm=TT, tn=512, name="fox_v")
    wf_pad = jnp.pad(wf[0], ((0, 0), (0, LANES - FOX_H)))
    bf_pad = jnp.pad(bf[0], (0, LANES - FOX_H)).reshape(1, LANES)
    f = _fox_logf(h, wf_pad, bf_pad)
    f3 = f.reshape(NB, TT, LANES)
    c3 = _fox_prompt_cum(f3)
    ct3 = jnp.swapaxes(c3[:, :, :FOX_H], 1, 2).reshape(NB, FOX_H, 1, TT)
    nrow = TS * FOX_H
    qs, ks, vs = (_sample_rows(z) for z in (q, k, v))
    fs = _sample_rows(f)[:, :, :FOX_H]
    new_f = jnp.pad(fs, ((0, 0), (0, PAGE - TS), (0, 0)))
    call = _fox_decode_cum(page_table, cache_f[0], new_f)
    crow = call[:, :N_PAGES].reshape(DEC_B, N_PAGES, 1, PAGE * FOX_H)
    cnew = call[:, N_PAGES, :TS]
    cq = jnp.broadcast_to(cnew.reshape(DEC_B, nrow, 1), (DEC_B, nrow, FOX_HD))
    o2 = _fox_decode_attn(page_table, qs.reshape(DEC_B, nrow, FOX_HD), cache_k[0], cache_v[0], crow, cq,
                          ks.reshape(DEC_B, nrow, FOX_HD), vs.reshape(DEC_B, nrow, FOX_HD),
                          cnew.reshape(DEC_B, 1, nrow))
    dec = o2.reshape(NB, 2 * TS, D)
    y = _fox_prompt(q, k, v, c3, ct3, dec)
    out = _matmul(y, wo, 0, tm=TT, tn=512, name="fox_o")
    heads = lambda z, t: z.reshape(-1, t, FOX_H, FOX_HD)
    outs = (heads(_prompt_rows(k), TP), heads(_prompt_rows(v), TP), _prompt_rows(f)[:, :, :FOX_H],
            heads(ks, TS), heads(vs, TS), fs)
    return out, outs


def kernel(x_prompt, x_sample, state_rwkv_wkv, state_rwkv_shift, state_ret, cache_fox_k, cache_fox_v, cache_fox_logf, state_ffn_conv, page_table, meta_tokens, norm_mix_pre, norm_mix_post, norm_ffn_pre, norm_ffn_post, rwkv_mu, rwkv_w0, rwkv_w1, rwkv_w2, rwkv_a0, rwkv_a1, rwkv_a2, rwkv_g1, rwkv_g2, rwkv_k_k, rwkv_k_a, rwkv_r_k, rwkv_wr, rwkv_wk, rwkv_wv, rwkv_wo, rwkv_lnx_g, rwkv_lnx_b, ret_wq, ret_wk, ret_wv, ret_wg, ret_wo, fox_wq, fox_wk, fox_wv, fox_wf, fox_bf, fox_wo, ffn_wgate, ffn_wup, ffn_conv_w, ffn_conv_b, ffn_wdown):
    meta = jnp.broadcast_to(meta_tokens[None].astype(F32), (NB, N_META, D))
    x = jnp.concatenate(
        [meta, x_prompt, x_sample.reshape(NB, 2 * TS, D), jnp.zeros((NB, TT - TP - 2 * TS, D), F32)], axis=1
    ).reshape(M, D)
    row3 = lambda p: p.reshape(p.shape[0], 1, p.shape[-1])
    g_mix_pre, g_mix_post, g_ffn_pre, g_ffn_post = (row3(p) for p in (norm_mix_pre, norm_mix_post, norm_ffn_pre, norm_ffn_post))
    g1p = jnp.pad(rwkv_g1, ((0, 0), (0, 0), (0, RWKV_LG_PAD - RWKV_LG)))
    g2p = jnp.pad(rwkv_g2, ((0, 0), (0, RWKV_LG_PAD - RWKV_LG), (0, 0)))
    wkv6 = state_rwkv_wkv.reshape(-1, NB, 2, RWKV_H, RWKV_N, RWKV_N)
    conv_state = jnp.concatenate(
        [jnp.zeros((DEPTH, NB, 1, 2, D_FF), F32), state_ffn_conv.reshape(DEPTH, NB, 2, 2, D_FF)], axis=2)
    cb3 = row3(ffn_conv_b)

    p_wkv, p_shift, s_wkv, s_shift, p_conv, s_conv = [], [], [], [], [], []
    h = None
    for i in range(DEPTH):
        kind = i % 3
        j = i // 3
        if kind == 0:
            out, (pw, ps, sw, ss) = _rwkv_layer(
                x, i, j, g_mix_pre, wkv6, state_rwkv_shift, rwkv_mu, rwkv_w0, rwkv_w1, rwkv_w2, rwkv_a0, rwkv_a1,
                rwkv_a2, g1p, g2p, rwkv_k_k, rwkv_k_a, rwkv_r_k, rwkv_wr, rwkv_wk, rwkv_wv, rwkv_wo, rwkv_lnx_g,
                rwkv_lnx_b)
            p_wkv.append(pw); p_shift.append(ps); s_wkv.append(sw); s_shift.append(ss)
        elif kind == 1:
            out, (p_ret, s_ret) = _ret_layer(h, state_ret, ret_wq, ret_wk, ret_wv, ret_wg, ret_wo)
        else:
            out, fox_outs = _fox_layer(h, cache_fox_k, cache_fox_v, cache_fox_logf, page_table, fox_wq, fox_wk,
                                       fox_wv, fox_wf, fox_bf, fox_wo)
        x, h = _resnorm(x, out, g_mix_post, i, g_ffn_pre, i)
        hm, cs = _ffn_up(h, ffn_wgate, ffn_wup, ffn_conv_w, cb3, conv_state[i], i)
        f = _matmul(hm, ffn_wdown, i, tm=TT, tn=256, kblock=(0, D_FF // 2), name="ffn_down_lo")
        f = _matmul(hm, ffn_wdown, i, tm=TT, tn=256, kblock=(1, D_FF // 2), acc=f, name="ffn_down_hi")
        p_conv.append(cs[:, 0]); s_conv.append(cs[:, 1:].reshape(DEC_B, 2, D_FF))
        if i + 1 < DEPTH and (i + 1) % 3 != 0:
            x, h = _resnorm(x, f, g_ffn_post, i, g_mix_pre, i + 1)
        else:
            x, h = _resnorm(x, f, g_ffn_post, i)
    x3 = x.reshape(NB, TT, D)
    y_prompt = x3[:, N_META:TP]
    y_sample = x3[:, TP:TP + 2 * TS].reshape(DEC_B, TS, D)
    p_fk, p_fv, p_ff, s_fk, s_fv, s_ff = fox_outs
    return (y_prompt, y_sample,
            jnp.stack(p_wkv), jnp.stack(p_shift), p_ret[None],
            p_fk[None], p_fv[None], p_ff[None], jnp.stack(p_conv),
            jnp.stack(s_wkv), jnp.stack(s_shift), s_ret[None],
            s_fk[None], s_fv[None], s_ff[None], jnp.stack(s_conv))
```

```python
import functools

import jax
import jax.numpy as jnp
from jax import lax
from jax.experimental import pallas as pl
from jax.experimental.pallas import tpu as pltpu

F32 = jnp.float32
BF16 = jnp.bfloat16
HIGHEST = lax.Precision.HIGHEST

D = 4096
NB = 4
SEQ = 2048
N_META = 16
TP = SEQ + N_META
DEC_B = 8
TS = 4
TT = 2080
SEQS = ((0, TP), (TP, TS), (TP + TS, TS))
M = NB * TT
DEPTH = 4
RMS_EPS = 1e-6

RWKV_N = 64
RWKV_H = D // RWKV_N
RWKV_GN_EPS = 64e-5
RWKV_L = 64
RWKV_PAIRS = 8
RWKV_LG = 480
RWKV_LG_PAD = 512

RET_H = 16
RET_DK = D // RET_H
RET_DV = 2 * RET_DK
RET_L = 128
ROPE_BASE = 10000.0
PAST_LEN = 8192

FOX_H = 32
FOX_HD = D // FOX_H
FOX_BQ = 512
PAGE = 128
N_PAGES = PAST_LEN // PAGE

D_FF = 11008
FFN_ROW_BLOCKS = 5

LANES = 128
V7X_VMEM_BYTES = 64 * 1024 * 1024
V7X_VMEM_CAP = 56 * 1024 * 1024

NT_DIMS = (((1,), (1,)), ((), ()))
TN_DIMS = (((0,), (0,)), ((), ()))


def _params(sems, vmem_bytes):
    return pltpu.CompilerParams(
        dimension_semantics=sems,
        vmem_limit_bytes=int(min(max(vmem_bytes * 5 // 4, 16 * 1024 * 1024), V7X_VMEM_CAP)),
    )


def _rms(x, g):
    return x * lax.rsqrt(jnp.mean(x * x, axis=-1, keepdims=True) + RMS_EPS) * g


ROW_TILE = 208


def _mix_body(x_ref, g_ref, mu_ref, sr_ref, o0, o1, o2, o3, o4, o5, hl_ref, carry, *, tr, nchunks):
    c = pl.program_id(1)

    @pl.when(c == 0)
    def _():
        carry[...] = jnp.zeros_like(carry)

    h = _rms(x_ref[...], g_ref[...])
    row = lax.broadcasted_iota(jnp.int32, (tr, 1), 0) + c * tr
    prev = pltpu.roll(h, 1, 0)
    prev = jnp.where(row == c * tr, carry[7:8, :], prev)
    for s, (start, _) in enumerate(SEQS):
        prev = jnp.where(row == start, sr_ref[0, s:s + 1, :], prev)
    carry[...] = h[tr - 8:, :]
    xx = prev - h
    for i, o in enumerate((o0, o1, o2, o3, o4, o5)):
        o[...] = (h + xx * mu_ref[i:i + 1, :]).astype(BF16)

    @pl.when(c == nchunks - 1)
    def _():
        hl_ref[0] = h[tr - 24:tr - 8, :]


def _rwkv_mix(x, g3, layer, mu, j, shift_rows):
    tr = ROW_TILE
    nchunks = TT // tr
    row_spec = pl.BlockSpec((tr, D), lambda b, c: (b * nchunks + c, 0))
    outs = pl.pallas_call(
        functools.partial(_mix_body, tr=tr, nchunks=nchunks),
        grid=(NB, nchunks),
        in_specs=[
            row_spec,
            pl.BlockSpec((None, 1, D), lambda b, c: (layer, 0, 0)),
            pl.BlockSpec((None, 6, D), lambda b, c: (j, 0, 0)),
            pl.BlockSpec((1, 8, D), lambda b, c: (b, 0, 0)),
        ],
        out_specs=[row_spec] * 6 + [pl.BlockSpec((1, 16, D), lambda b, c: (b, 0, 0))],
        out_shape=[jax.ShapeDtypeStruct((M, D), BF16)] * 6 + [jax.ShapeDtypeStruct((NB, 16, D), F32)],
        scratch_shapes=[pltpu.VMEM((8, D), F32)],
        compiler_params=_params(("arbitrary", "arbitrary"), 48 * 1024 * 1024),
        name="rwkv_mix",
    )(x, g3, mu, shift_rows)
    return outs[:6], outs[6]


def _resnorm_body(res_ref, y_ref, gp_ref, gn_ref, xo_ref, ho_ref):
    x = res_ref[...] + _rms(y_ref[...], gp_ref[...])
    xo_ref[...] = x
    ho_ref[...] = _rms(x, gn_ref[...]).astype(BF16)


def _res_body(res_ref, y_ref, gp_ref, xo_ref):
    xo_ref[...] = res_ref[...] + _rms(y_ref[...], gp_ref[...])


def _resnorm(res, y, gpost3, lp, gnext3=None, ln=0):
    tr = ROW_TILE
    row_spec = pl.BlockSpec((tr, D), lambda i: (i, 0))
    gspec = lambda l: pl.BlockSpec((None, 1, D), lambda i: (l, 0, 0))
    if gnext3 is None:
        return pl.pallas_call(
            _res_body, grid=(M // tr,), in_specs=[row_spec, row_spec, gspec(lp)], out_specs=row_spec,
            out_shape=jax.ShapeDtypeStruct((M, D), F32),
            compiler_params=_params(("arbitrary",), 32 * 1024 * 1024), name="res_add",
        )(res, y, gpost3), None
    return pl.pallas_call(
        _resnorm_body, grid=(M // tr,), in_specs=[row_spec, row_spec, gspec(lp), gspec(ln)],
        out_specs=[row_spec, row_spec],
        out_shape=[jax.ShapeDtypeStruct((M, D), F32), jax.ShapeDtypeStruct((M, D), BF16)],
        compiler_params=_params(("arbitrary",), 40 * 1024 * 1024), name="res_norm",
    )(res, y, gpost3, gnext3)


def _mm_body(x_ref, w_ref, o_ref):
    o_ref[...] = jnp.dot(x_ref[...], w_ref[...].astype(BF16), preferred_element_type=F32).astype(o_ref.dtype)


def _mm_acc_body(x_ref, w_ref, a_ref, o_ref):
    o_ref[...] = a_ref[...] + jnp.dot(x_ref[...], w_ref[...].astype(BF16), preferred_element_type=F32)


def _matmul(x, w, layer, *, tm=TT // 2, tn=512, out_dtype=F32, name, kblock=None, acc=None):
    m, k = x.shape
    n = w.shape[-1]
    kidx, ksz = (0, k) if kblock is None else kblock
    osz = jnp.dtype(out_dtype).itemsize
    vmem = 2 * tm * ksz * 2 + 2 * ksz * tn * 4 + ksz * tn * 2 + 2 * tm * tn * osz + tm * tn * 4 + (4 << 20)
    in_specs = [
        pl.BlockSpec((tm, ksz), lambda i, j: (i, kidx)),
        pl.BlockSpec((None, ksz, tn), lambda i, j: (layer, kidx, j)),
    ]
    out_spec = pl.BlockSpec((tm, tn), lambda i, j: (i, j))
    args = (x, w)
    body = _mm_body
    aliases = {}
    if acc is not None:
        in_specs.append(out_spec)
        args = (x, w, acc)
        body = _mm_acc_body
        aliases = {2: 0}
        vmem += 2 * tm * tn * 4
    return pl.pallas_call(
        body,
        grid=(m // tm, n // tn),
        in_specs=in_specs,
        out_specs=out_spec,
        out_shape=jax.ShapeDtypeStruct((m, n), out_dtype),
        input_output_aliases=aliases,
        compiler_params=_params(("arbitrary", "arbitrary"), vmem),
        name=name,
    )(*args)


def _lora_body(*refs, kind):
    x_ref, w1_ref, w2_ref = refs[:3]
    o_ref = refs[-1]
    t = jnp.dot(x_ref[...], w1_ref[...].astype(BF16), preferred_element_type=F32)
    if kind == "decay":
        t = jnp.tanh(t)
    elif kind == "gate":
        t = jax.nn.sigmoid(t)
    z = jnp.dot(t.astype(BF16), w2_ref[...].astype(BF16), preferred_element_type=F32)
    if kind == "decay":
        z = -jnp.exp(-jax.nn.softplus(-(refs[3][...] + z)) - 0.5)
    elif kind == "iclr":
        z = jax.nn.sigmoid(refs[3][...] + z)
    o_ref[...] = z.astype(o_ref.dtype)


def _lora(x, w1, w2, j, *, kind, out_dtype, bias3=None, tm=416):
    m, k = x.shape
    r = w1.shape[-1]
    n = w2.shape[-1]
    vmem = 2 * tm * k * 2 + 3 * (k * r + r * n) * 4 + 3 * tm * n * 4 + (4 << 20)
    in_specs = [
        pl.BlockSpec((tm, k), lambda i: (i, 0)),
        pl.BlockSpec((None, k, r), lambda i: (j, 0, 0), pipeline_mode=pl.Buffered(1)),
        pl.BlockSpec((None, r, n), lambda i: (j, 0, 0), pipeline_mode=pl.Buffered(1)),
    ]
    args = (x, w1, w2)
    if bias3 is not None:
        in_specs.append(pl.BlockSpec((None, 1, n), lambda i: (j, 0, 0)))
        args += (bias3,)
    return pl.pallas_call(
        functools.partial(_lora_body, kind=kind),
        grid=(m // tm,),
        in_specs=in_specs,
        out_specs=pl.BlockSpec((tm, n), lambda i: (i, 0)),
        out_shape=jax.ShapeDtypeStruct((m, n), out_dtype),
        compiler_params=_params(("arbitrary",), vmem),
        name="rwkv_lora_" + kind,
    )(*args)


def _ffn_up_body(x_ref, wg_ref, wu_ref, cw_ref, cb_ref, st_ref, h_ref, so_ref):
    wg = wg_ref[...].astype(BF16)
    wu = wu_ref[...].astype(BF16)
    cw = cw_ref[...]
    nblk = FFN_ROW_BLOCKS
    hr = TT // nblk
    gs, us = [], []
    for r in range(nblk):
        x = x_ref[r * hr:(r + 1) * hr, :]
        gs.append(jnp.dot(x, wg, preferred_element_type=F32))
        us.append(jnp.dot(x, wu, preferred_element_type=F32))
    for r in range(nblk):
        g = gs[r]
        base = r * hr
        row = lax.broadcasted_iota(jnp.int32, (hr, 1), 0) + base
        s1 = pltpu.roll(g, 1, 0)
        s2 = pltpu.roll(g, 2, 0)
        if r > 0:
            last = gs[r - 1][hr - 2:hr, :]
            s1 = jnp.where(row == base, last[1:2, :], s1)
            s2 = jnp.where(row == base, last[0:1, :], jnp.where(row == base + 1, last[1:2, :], s2))
        for s, (start, length) in enumerate(SEQS):
            st = st_ref[0, s]
            if base <= start < base + hr:
                s1 = jnp.where(row == start, st[1:2, :], s1)
                s2 = jnp.where(row == start, st[0:1, :], s2)
            if base <= start + 1 < base + hr:
                s2 = jnp.where(row == start + 1, st[1:2, :], s2)
            end = start + length
            if base < end <= base + hr:
                so_ref[0, s] = g[end - 2 - base:end - base, :]
        c = cb_ref[...] + cw[0:1, :] * s2 + cw[1:2, :] * s1 + cw[2:3, :] * g
        act = 0.5 * c * (1.0 + lax.erf(c * (2.0 ** -0.5)))
        h_ref[base:base + hr, :] = (act * us[r]).astype(BF16)


def _ffn_up(h, wg, wu, cw, cb3, state, layer, *, tn=256):
    nj = D_FF // tn
    vmem = TT * D * 2 + 4 * D * tn * 4 + 2 * D * tn * 2 + 2 * TT * tn * 2 + 6 * TT * tn * 4 + (4 << 20)
    return pl.pallas_call(
        _ffn_up_body,
        grid=(NB, nj),
        in_specs=[
            pl.BlockSpec((TT, D), lambda b, j: (b, 0), pipeline_mode=pl.Buffered(1)),
            pl.BlockSpec((None, D, tn), lambda b, j: (layer, 0, j)),
            pl.BlockSpec((None, D, tn), lambda b, j: (layer, 0, j)),
            pl.BlockSpec((None, 3, tn), lambda b, j: (layer, 0, j)),
            pl.BlockSpec((None, 1, tn), lambda b, j: (layer, 0, j)),
            pl.BlockSpec((1, 3, 2, tn), lambda b, j: (b, 0, 0, j)),
        ],
        out_specs=[
            pl.BlockSpec((TT, tn), lambda b, j: (b, j)),
            pl.BlockSpec((1, 3, 2, tn), lambda b, j: (b, 0, 0, j)),
        ],
        out_shape=[jax.ShapeDtypeStruct((M, D_FF), BF16), jax.ShapeDtypeStruct((NB, 3, 2, D_FF), F32)],
        compiler_params=_params(("arbitrary", "arbitrary"), vmem),
        name="ffn_up",
    )(h, wg, wu, cw, cb3, state)


def _rwkv_consts():
    n = 2 * RWKV_L
    lane = lax.broadcasted_iota(jnp.int32, (1, LANES), 1)
    ri = lax.broadcasted_iota(jnp.int32, (n, n), 0)
    ci = lax.broadcasted_iota(jnp.int32, (n, n), 1)
    same = (ri // RWKV_L) == (ci // RWKV_L)
    return dict(
        m0=lane < RWKV_N,
        strict=jnp.logical_and(same, ci < ri),
        incl=jnp.logical_and(same, ci <= ri),
        eye=(ri == ci).astype(F32),
        row=lax.broadcasted_iota(jnp.int32, (RWKV_L, 1), 0),
    )


def _rwkv_prep(r, k, v, ld, a, kk_c, ka_c, cst, active):
    L = RWKV_L
    m0 = cst["m0"]

    def seg(z):
        s0 = jnp.sum(jnp.where(m0, z, 0.0), axis=-1, keepdims=True)
        s1 = jnp.sum(jnp.where(m0, 0.0, z), axis=-1, keepdims=True)
        return jnp.where(m0, s0, s1)

    kraw = k * kk_c
    kk = kraw * lax.rsqrt(seg(kraw * kraw) + 1e-12)
    kmod = k * (1.0 + (a - 1.0) * ka_c)
    aa = -kk
    bb = kk * a
    km = kmod
    vv = v
    if active is not None:
        ld, aa, bb, km, vv = (jnp.where(active, z, 0.0) for z in (ld, aa, bb, km, vv))
    cum = ld
    shift = 1
    while shift < L:
        cum = cum + jnp.where(cst["row"] >= shift, pltpu.roll(cum, shift, 0), 0.0)
        shift *= 2
    cum_l = cum[L - 1:L, :]
    e_neg = jnp.exp(-cum)
    e_rem = jnp.exp(cum_l - cum)

    def stack(z):
        return jnp.concatenate([jnp.where(m0, z, 0.0), jnp.where(m0, 0.0, z)], axis=0)

    rt = stack(r * jnp.exp(cum))
    a2 = stack(aa * jnp.exp(cum - ld)).astype(BF16)
    return dict(
        rt=rt, a2=a2, xa=jnp.concatenate([a2, rt.astype(BF16)], axis=0),
        b2=stack(bb * e_neg).astype(BF16), k2=stack(km * e_neg).astype(BF16),
        bh2=stack(bb * e_rem).astype(BF16), kh2=stack(km * e_rem).astype(BF16),
        v2=stack(vv).astype(BF16), decay=jnp.exp(cum_l), kmod=kmod)


def _rwkv_chunks(preps, ws, cst):
    L = RWKV_L
    n = 2 * L
    dot = lambda p, q: jnp.dot(p, q, preferred_element_type=F32)
    nt = lambda p, q: lax.dot_general(p, q, NT_DIMS, preferred_element_type=F32)
    tn = lambda p, q: lax.dot_general(p, q, TN_DIMS, preferred_element_type=F32)
    xb = [nt(p["xa"], p["b2"]) for p in preps]
    xk = [nt(p["xa"], p["k2"]) for p in preps]
    a_ab = [jnp.where(cst["strict"], z[:n], 0.0) for z in xb]
    a_rb = [jnp.where(cst["incl"], z[n:], 0.0).astype(BF16) for z in xb]
    a_ak = [jnp.where(cst["strict"], z[:n], 0.0).astype(BF16) for z in xk]
    a_rk = [jnp.where(cst["incl"], z[n:], 0.0).astype(BF16) for z in xk]
    akv = [dot(m, p["v2"]) for m, p in zip(a_ak, preps)]
    tinv = [cst["eye"] + z for z in a_ab]
    q = a_ab
    for _ in range(max(L.bit_length() - 2, 0)):
        qb = [z.astype(BF16) for z in q]
        q = [dot(z, z) for z in qb]
        tinv = [t + dot(t.astype(BF16), z.astype(BF16)) for t, z in zip(tinv, q)]
    tu = [dot(t.astype(BF16), jnp.concatenate([u.astype(BF16), p["a2"]], axis=1)).astype(BF16)
          for t, u, p in zip(tinv, akv, preps)]
    ru = [dot(m, z) for m, z in zip(a_rb, tu)]
    rkv = [dot(m, p["v2"]) for m, p in zip(a_rk, preps)]
    bu = [tn(p["bh2"], z) for p, z in zip(preps, tu)]
    kv = [tn(p["kh2"], p["v2"]) for p in preps]
    wh = [w.astype(BF16) for w in ws]
    o2 = [dot((p["rt"] + z[:, LANES:]).astype(BF16), w) + (z[:, :LANES] + y)
          for p, z, y, w in zip(preps, ru, rkv, wh)]
    phi = [cst["eye"] * p["decay"] + z[:, LANES:] for p, z in zip(preps, bu)]
    ph = [z.astype(BF16) for z in phi]
    pl_ = [(z - zh.astype(F32)).astype(BF16) for z, zh in zip(phi, ph)]
    wl = [(w - zh.astype(F32)).astype(BF16) for w, zh in zip(ws, wh)]
    hh = [dot(x, y) for x, y in zip(ph, wh)]
    hl = [dot(x, y) for x, y in zip(ph, wl)]
    lh = [dot(x, y) for x, y in zip(pl_, wh)]
    w_new = [x + (y + z) + (b[:, :LANES] + c) for x, y, z, b, c in zip(hh, hl, lh, bu, kv)]
    return [(z[:L] + z[L:], w) for z, w in zip(o2, w_new)]


def _rwkv_body(r_ref, k_ref, v_ref, g_ref, ld_ref, a_ref, kk_ref, ka_ref, rk_ref, lg_ref, lb_ref, s0_ref,
               y_ref, so_ref, *, npairs):
    L = RWKV_L
    nfull = TP // L
    cst = _rwkv_consts()
    m0 = cst["m0"]

    def seg(z):
        s0 = jnp.sum(jnp.where(m0, z, 0.0), axis=-1, keepdims=True)
        s1 = jnp.sum(jnp.where(m0, 0.0, z), axis=-1, keepdims=True)
        return jnp.where(m0, s0, s1)

    def run(items):
        loaded, preps = [], []
        for p, rows, _, active in items:
            ln = slice(LANES * p, LANES * (p + 1))
            r = r_ref[rows, ln].astype(F32)
            v = v_ref[rows, ln].astype(F32)
            loaded.append((ln, r, v))
            preps.append(_rwkv_prep(r, k_ref[rows, ln].astype(F32), v, ld_ref[rows, ln], a_ref[rows, ln].astype(F32),
                                    kk_ref[:, ln], ka_ref[:, ln], cst, active))
        outs = []
        for (p, rows, _, _), (ln, r, v), prep, (o, w_new) in zip(
                items, loaded, preps, _rwkv_chunks(preps, [it[2] for it in items], cst)):
            mean = seg(o) * (1.0 / RWKV_N)
            xc = o - mean
            var = seg(xc * xc) * (1.0 / RWKV_N)
            on = xc * lax.rsqrt(var + RWKV_GN_EPS) * lg_ref[:, ln] + lb_ref[:, ln]
            bonus = seg(r * prep["kmod"] * rk_ref[:, ln]) * v
            outs.append(((on + bonus) * g_ref[rows, ln].astype(F32), w_new))
        return outs

    def body(c, ws):
        rows = pl.ds(pl.multiple_of(c * L, L), L)
        outs = run([(p, rows, ws[p], None) for p in range(npairs)])
        for p, (y, _) in enumerate(outs):
            y_ref[rows, LANES * p:LANES * (p + 1)] = y.astype(BF16)
        return tuple(w for _, w in outs)

    ws = lax.fori_loop(0, nfull, body, tuple(jnp.zeros((LANES, LANES), F32) for _ in range(npairs)))

    win = slice(TT - L, TT)
    lrow = lax.broadcasted_iota(jnp.int32, (L, 1), 0) + (TT - L)
    zero = jnp.zeros((RWKV_N, RWKV_N), F32)
    ytail = [jnp.zeros((L, LANES), F32) for _ in range(npairs)]
    for s, (start, length) in enumerate(SEQS):
        active = jnp.logical_and(lrow >= max(start, nfull * L), lrow < start + length)
        items = []
        for p in range(npairs):
            if s == 0:
                w0 = ws[p]
            else:
                s_in = s0_ref[0, s - 1, 2 * p:2 * p + 2]
                st = jnp.concatenate([jnp.concatenate([s_in[0], zero], axis=1),
                                      jnp.concatenate([zero, s_in[1]], axis=1)], axis=0)
                w0 = st.T
            items.append((p, win, w0, active))
        for p, (y, w_new) in enumerate(run(items)):
            ytail[p] = jnp.where(active, y, ytail[p])
            wt = w_new.T
            so_ref[0, s, 2 * p] = wt[:RWKV_N, :RWKV_N]
            so_ref[0, s, 2 * p + 1] = wt[RWKV_N:, RWKV_N:]
    for p in range(npairs):
        y_ref[nfull * L:TT, LANES * p:LANES * (p + 1)] = ytail[p][nfull * L - (TT - L):].astype(BF16)


def _rwkv(r, k, v, g, ld, a, kk3, ka3, rk3, lg3, lb3, s0, j, *, npairs=RWKV_PAIRS):
    pw = LANES * npairs
    nh = 2 * npairs
    act = lambda: pl.BlockSpec((TT, pw), lambda b, p: (b, p), pipeline_mode=pl.Buffered(1))
    out = pl.BlockSpec((TT, pw), lambda b, p: (b, p))
    par = lambda: pl.BlockSpec((None, 1, pw), lambda b, p: (j, 0, p))
    vmem = TT * pw * (5 * 2 + 4) + 2 * TT * pw * 2 + (16 << 20)
    return pl.pallas_call(
        functools.partial(_rwkv_body, npairs=npairs),
        grid=(NB, RWKV_H // nh),
        in_specs=[act(), act(), act(), act(), act(), act(), par(), par(), par(), par(), par(),
                  pl.BlockSpec((None, 1, 2, nh, RWKV_N, RWKV_N), lambda b, p: (j, b, 0, p, 0, 0))],
        out_specs=[out, pl.BlockSpec((1, 3, nh, RWKV_N, RWKV_N), lambda b, p: (b, 0, p, 0, 0))],
        out_shape=[jax.ShapeDtypeStruct((M, D), BF16),
                   jax.ShapeDtypeStruct((NB, 3, RWKV_H, RWKV_N, RWKV_N), F32)],
        compiler_params=_params(("arbitrary", "arbitrary"), vmem),
        name="rwkv7",
    )(r, k, v, g, ld, a, kk3, ka3, rk3, lg3, lb3, s0)


def _ret_body(q_ref, k_ref, v_ref, g_ref, cos_ref, sin_ref, lg_ref, s0_ref, y_ref, so_ref):
    L = RET_L
    nfull = TP // L
    half = RET_DK // 2
    lg = lg_ref[:, :1]
    ii = lax.broadcasted_iota(jnp.int32, (L, L), 0)
    jj = lax.broadcasted_iota(jnp.int32, (L, L), 1)
    ri = lax.broadcasted_iota(jnp.int32, (L, 1), 0)
    dfull = jnp.where(jj <= ii, jnp.exp((ii - jj).astype(F32) * lg), 0.0)

    def rot(z, cs, sn):
        z1 = z[:, :half]
        z2 = z[:, half:]
        return jnp.concatenate([z1 * cs - z2 * sn, z1 * sn + z2 * cs], axis=1)

    def chunk(rows, s_in, off, length):
        cs = cos_ref[rows, :]
        sn = sin_ref[rows, :]
        q = rot(q_ref[rows, :], cs, sn).astype(BF16)
        k = rot(k_ref[rows, :], cs, sn) * (RET_DK ** -0.5)
        v = v_ref[rows, :]
        nloc = (ri - off).astype(F32)
        act = jnp.logical_and(ri >= off, ri < off + length)
        if off == 0 and length == L:
            dmat = dfull
        else:
            dmat = jnp.where(jnp.logical_and(jj >= off, ii < off + length), dfull, 0.0)
        inner = jnp.exp((nloc + 1.0) * lg)
        kdec = jnp.where(act, jnp.exp((length - 1.0 - nloc) * lg), 0.0)
        sc = lax.dot_general(q, k.astype(BF16), NT_DIMS, preferred_element_type=F32) * dmat
        o = jnp.dot(sc.astype(BF16), v, preferred_element_type=F32)
        o = o + jnp.dot(q, s_in.astype(BF16), preferred_element_type=F32) * inner
        s_out = jnp.exp(length * lg) * s_in + lax.dot_general((k * kdec).astype(BF16), v, TN_DIMS,
                                                                preferred_element_type=F32)
        o = o * lax.rsqrt(jnp.mean(o * o, axis=-1, keepdims=True) + RMS_EPS)
        gate = g_ref[rows, :].astype(F32)
        return gate * jax.nn.sigmoid(gate) * o, s_out, act

    so_ref[0, 0, 0] = jnp.zeros((RET_DK, RET_DV), F32)

    def body(c, carry):
        rows = pl.ds(pl.multiple_of(c * L, L), L)
        y, s_out, _ = chunk(rows, so_ref[0, 0, 0], 0, L)
        y_ref[rows, :] = y.astype(BF16)
        so_ref[0, 0, 0] = s_out
        return carry

    lax.fori_loop(0, nfull, body, 0)

    win = slice(TT - L, TT)
    ytail = jnp.zeros((L, RET_DV), F32)
    for s, (start, length) in enumerate(SEQS):
        lo = max(start, nfull * L)
        s_in = so_ref[0, 0, 0] if s == 0 else s0_ref[0, s - 1, 0]
        y, s_out, act = chunk(win, s_in, lo - (TT - L), start + length - lo)
        ytail = jnp.where(act, y, ytail)
        so_ref[0, s, 0] = s_out
    y_ref[nfull * L:TT, :] = ytail[nfull * L - (TT - L):].astype(BF16)


def _retention(q, k, v, g, cos, sin, lg3, s0):
    spec = lambda w: pl.BlockSpec((TT, w), lambda b, h: (b, h))
    tab = pl.BlockSpec((TT, RET_DK // 2), lambda b, h: (0, 0))
    vmem = 2 * TT * (2 * RET_DK * 4 + 2 * RET_DV * 2 + RET_DV * 2 + 2 * 128 * 4) + 12 * RET_DK * RET_DV * 4 + (8 << 20)
    return pl.pallas_call(
        _ret_body,
        grid=(NB, RET_H),
        in_specs=[spec(RET_DK), spec(RET_DK), spec(RET_DV), spec(RET_DV), tab, tab,
                  pl.BlockSpec((None, 1, LANES), lambda b, h: (h, 0, 0)),
                  pl.BlockSpec((1, 2, 1, RET_DK, RET_DV), lambda b, h: (b, 0, h, 0, 0))],
        out_specs=[spec(RET_DV), pl.BlockSpec((1, 3, 1, RET_DK, RET_DV), lambda b, h: (b, 0, h, 0, 0))],
        out_shape=[jax.ShapeDtypeStruct((M, RET_H * RET_DV), BF16),
                   jax.ShapeDtypeStruct((NB, 3, RET_H, RET_DK, RET_DV), F32)],
        compiler_params=_params(("arbitrary", "arbitrary"), vmem),
        name="retention",
    )(q, k, v, g, cos, sin, lg3, s0)


def _fproj_body(x_ref, w_ref, b_ref, o_ref):
    z = jnp.dot(x_ref[...], w_ref[...].astype(BF16), preferred_element_type=F32) + b_ref[...]
    o_ref[...] = jax.nn.log_sigmoid(z)


def _fox_logf(h, wf_pad, bf_pad, *, tm=416):
    return pl.pallas_call(
        _fproj_body, grid=(M // tm,),
        in_specs=[pl.BlockSpec((tm, D), lambda i: (i, 0)), pl.BlockSpec((D, LANES), lambda i: (0, 0)),
                  pl.BlockSpec((1, LANES), lambda i: (0, 0))],
        out_specs=pl.BlockSpec((tm, LANES), lambda i: (i, 0)),
        out_shape=jax.ShapeDtypeStruct((M, LANES), F32),
        compiler_params=_params(("arbitrary",), 24 * 1024 * 1024), name="fox_logf",
    )(h, wf_pad, bf_pad)


def _tri(n):
    return (lax.broadcasted_iota(jnp.int32, (n, n), 1) <= lax.broadcasted_iota(jnp.int32, (n, n), 0)).astype(F32)


def _pcum_body(f_ref, c_ref):
    n = LANES
    tri = _tri(n)
    carry = jnp.zeros((1, LANES), F32)
    nfull = TP // n
    for blk in range(nfull):
        cb = jnp.dot(tri, f_ref[0, blk * n:(blk + 1) * n, :], precision=HIGHEST, preferred_element_type=F32) + carry
        c_ref[0, blk * n:(blk + 1) * n, :] = cb
        carry = cb[n - 1:n, :]
    row = lax.broadcasted_iota(jnp.int32, (n, 1), 0) + (TT - n)
    x = jnp.where(row >= nfull * n, f_ref[0, TT - n:TT, :], 0.0)
    cb = jnp.dot(tri, x, precision=HIGHEST, preferred_element_type=F32) + carry
    c_ref[0, nfull * n:TT, :] = cb[nfull * n - (TT - n):, :]


def _fox_prompt_cum(f3):
    spec = pl.BlockSpec((1, TT, LANES), lambda b: (b, 0, 0))
    return pl.pallas_call(
        _pcum_body, grid=(NB,), in_specs=[spec], out_specs=spec,
        out_shape=jax.ShapeDtypeStruct((NB, TT, LANES), F32),
        compiler_params=_params(("arbitrary",), 16 * 1024 * 1024), name="fox_cum",
    )(f3)


def _softmax_step(s, vb, m, l, acc):
    m_new = jnp.maximum(m, jnp.max(s, axis=-1, keepdims=True))
    alpha = jnp.exp(m - m_new)
    p = jnp.exp(s - m_new)
    l = alpha * l + jnp.sum(p, axis=-1, keepdims=True)
    acc = alpha * acc + jnp.dot(p.astype(BF16), vb, preferred_element_type=F32)
    return m_new, l, acc


def _fox_body(q_ref, k_ref, v_ref, c_ref, ct_ref, dec_ref, y_ref):
    h = pl.program_id(1)
    bq = FOX_BQ
    nq = TP // bq
    scale = FOX_HD ** -0.5
    pick = (lax.broadcasted_iota(jnp.int32, (LANES, LANES), 0) == h).astype(F32)

    def attend(qrows, nrows, kv_blocks):
        q = (q_ref[qrows, :] * scale).astype(BF16)
        cq = jnp.dot(c_ref[0, qrows, :], pick, precision=HIGHEST, preferred_element_type=F32)[:, :1]
        m = jnp.full((nrows, 1), -jnp.inf, F32)
        l = jnp.zeros((nrows, 1), F32)
        acc = jnp.zeros((nrows, FOX_HD), F32)
        for krows, mask in kv_blocks:
            s = lax.dot_general(q, k_ref[krows, :].astype(BF16), NT_DIMS, preferred_element_type=F32)
            s = s + cq - ct_ref[:, krows]
            if mask is not None:
                s = jnp.where(mask, s, -jnp.inf)
            m, l, acc = _softmax_step(s, v_ref[krows, :].astype(BF16), m, l, acc)
        return acc / l

    rr = lax.broadcasted_iota(jnp.int32, (bq, bq), 0)
    cc = lax.broadcasted_iota(jnp.int32, (bq, bq), 1)
    causal = cc <= rr
    blocks = [slice(i * bq, (i + 1) * bq) for i in range(nq)]
    for qi in range(nq):
        kv = [(blocks[kj], causal if kj == qi else None) for kj in range(qi + 1)]
        y_ref[blocks[qi], :] = attend(blocks[qi], bq, kv).astype(BF16)
    nt = TT - nq * bq
    qrow = lax.broadcasted_iota(jnp.int32, (nt, LANES), 0) + nq * bq
    kcol = lax.broadcasted_iota(jnp.int32, (nt, LANES), 1) + (TT - LANES)
    tail_mask = jnp.logical_and(kcol >= nq * bq, kcol <= qrow)
    kv = [(b, None) for b in blocks] + [(slice(TT - LANES, TT), tail_mask)]
    o = attend(slice(nq * bq, TT), nt, kv)
    npr = TP - nq * bq
    y_ref[nq * bq:TT, :] = jnp.concatenate(
        [o[:npr], dec_ref[0], jnp.zeros((TT - TP - 2 * TS, FOX_HD), F32)], axis=0).astype(BF16)


def _fox_prompt(q, k, v, c3, ct3, dec):
    spec = pl.BlockSpec((TT, FOX_HD), lambda b, h: (b, h))
    return pl.pallas_call(
        _fox_body,
        grid=(NB, FOX_H),
        in_specs=[spec, spec, spec,
                  pl.BlockSpec((1, TT, LANES), lambda b, h: (b, 0, 0)),
                  pl.BlockSpec((None, None, 1, TT), lambda b, h: (b, h, 0, 0)),
                  pl.BlockSpec((1, 2 * TS, FOX_HD), lambda b, h: (b, 0, h))],
        out_specs=spec,
        out_shape=jax.ShapeDtypeStruct((M, D), BF16),
        compiler_params=_params(("arbitrary", "arbitrary"), 40 * 1024 * 1024),
        name="fox_prompt",
    )(q, k, v, c3, ct3, dec)


def _dcum_body(pt_ref, cf_ref, nf_ref, o_ref, carry):
    del pt_ref
    p = pl.program_id(1)

    @pl.when(p == 0)
    def _():
        carry[...] = jnp.zeros_like(carry)

    x = jnp.where(p == N_PAGES, nf_ref[0], cf_ref[0])
    cb = jnp.dot(_tri(PAGE), x, precision=HIGHEST, preferred_element_type=F32) + carry[...]
    o_ref[0, 0] = cb
    carry[...] = cb[PAGE - 1:PAGE, :]


def _fox_decode_cum(page_table, cache_f, new_f):
    grid_spec = pltpu.PrefetchScalarGridSpec(
        num_scalar_prefetch=1,
        grid=(DEC_B, N_PAGES + 1),
        in_specs=[
            pl.BlockSpec((1, PAGE, FOX_H), lambda b, p, pt: (pt[b, jnp.minimum(p, N_PAGES - 1)], 0, 0)),
            pl.BlockSpec((1, PAGE, FOX_H), lambda b, p, pt: (b, 0, 0)),
        ],
        out_specs=pl.BlockSpec((1, 1, PAGE, FOX_H), lambda b, p, pt: (b, p, 0, 0)),
        scratch_shapes=[pltpu.VMEM((1, FOX_H), F32)],
    )
    return pl.pallas_call(
        _dcum_body, grid_spec=grid_spec,
        out_shape=jax.ShapeDtypeStruct((DEC_B, N_PAGES + 1, PAGE, FOX_H), F32),
        compiler_params=_params(("arbitrary", "arbitrary"), 16 * 1024 * 1024), name="fox_decode_cum",
    )(page_table, cache_f, new_f)


def _dattn_body(pt_ref, q_ref, ck_ref, cv_ref, crow_ref, cq_ref, kn_ref, vn_ref, cn_ref, o_ref, m_s, l_s, acc_s):
    del pt_ref
    p = pl.program_id(1)
    nrow = TS * FOX_H

    @pl.when(p == 0)
    def _():
        m_s[...] = jnp.full_like(m_s, -jnp.inf)
        l_s[...] = jnp.zeros_like(l_s)
        acc_s[...] = jnp.zeros_like(acc_s)

    q = (q_ref[0] * (FOX_HD ** -0.5)).astype(BF16)
    cq = cq_ref[0][:, :1]
    hrow = lax.broadcasted_iota(jnp.int32, (nrow, 1), 0) % FOX_H

    def update(s, vb):
        m, l, acc = _softmax_step(s, vb, m_s[...], l_s[...], acc_s[...])
        m_s[...] = m
        l_s[...] = l
        acc_s[...] = acc

    @pl.when(p < N_PAGES)
    def _():
        kb = ck_ref[0].reshape(PAGE * FOX_H, FOX_HD).astype(BF16)
        vb = cv_ref[0].reshape(PAGE * FOX_H, FOX_HD).astype(BF16)
        s = lax.dot_general(q, kb, NT_DIMS, preferred_element_type=F32) + cq - crow_ref[0, 0]
        hcol = lax.broadcasted_iota(jnp.int32, (1, PAGE * FOX_H), 1) % FOX_H
        update(jnp.where(hcol == hrow, s, -jnp.inf), vb)

    @pl.when(p == N_PAGES)
    def _():
        s = lax.dot_general(q, kn_ref[0].astype(BF16), NT_DIMS, preferred_element_type=F32) + cq - cn_ref[0]
        col = lax.broadcasted_iota(jnp.int32, (1, nrow), 1)
        row = lax.broadcasted_iota(jnp.int32, (nrow, 1), 0)
        valid = jnp.logical_and(col % FOX_H == hrow, col // FOX_H <= row // FOX_H)
        update(jnp.where(valid, s, -jnp.inf), vn_ref[0].astype(BF16))
        o_ref[0] = acc_s[...] / l_s[...]


def _fox_decode_attn(page_table, q2, cache_k, cache_v, crow, cq, kn, vn, cn):
    nrow = TS * FOX_H
    page = lambda b, p, pt: (pt[b, jnp.minimum(p, N_PAGES - 1)], 0, 0, 0)
    per_b = pl.BlockSpec((1, nrow, FOX_HD), lambda b, p, pt: (b, 0, 0))
    grid_spec = pltpu.PrefetchScalarGridSpec(
        num_scalar_prefetch=1,
        grid=(DEC_B, N_PAGES + 1),
        in_specs=[
            per_b,
            pl.BlockSpec((1, PAGE, FOX_H, FOX_HD), page),
            pl.BlockSpec((1, PAGE, FOX_H, FOX_HD), page),
            pl.BlockSpec((1, 1, 1, PAGE * FOX_H), lambda b, p, pt: (b, jnp.minimum(p, N_PAGES - 1), 0, 0)),
            per_b, per_b, per_b,
            pl.BlockSpec((1, 1, nrow), lambda b, p, pt: (b, 0, 0)),
        ],
        out_specs=per_b,
        scratch_shapes=[pltpu.VMEM((nrow, 1), F32), pltpu.VMEM((nrow, 1), F32), pltpu.VMEM((nrow, FOX_HD), F32)],
    )
    return pl.pallas_call(
        _dattn_body, grid_spec=grid_spec,
        out_shape=jax.ShapeDtypeStruct((DEC_B, nrow, FOX_HD), F32),
        compiler_params=_params(("arbitrary", "arbitrary"), 40 * 1024 * 1024), name="fox_decode_attn",
    )(page_table, q2, cache_k, cache_v, crow, cq, kn, vn, cn)


def _sample_rows(a):
    n = a.shape[-1]
    return a.reshape(NB, TT, n)[:, TP:TP + 2 * TS].reshape(DEC_B, TS, n)


def _prompt_rows(a):
    n = a.shape[-1]
    return a.reshape(NB, TT, n)[:, :TP]


def _pack_states(prompt_like_zero, sample):
    s = sample.reshape((NB, 2) + sample.shape[1:])
    return jnp.concatenate([jnp.zeros_like(s[:, :1]) if prompt_like_zero else s[:, :0], s], axis=1)


def _rwkv_layer(x, layer, j, gpre3, state_wkv6, state_shift, mu, w0, w1, w2, a0, a1, a2, g1p, g2p, k_k, k_a, r_k,
                wr, wk, wv, wo, lnx_g, lnx_b):
    shift_rows = jnp.concatenate(
        [jnp.zeros((NB, 1, D), F32), state_shift[j].reshape(NB, 2, D), jnp.zeros((NB, 5, D), F32)], axis=1)
    (xr, xw, xk, xv, xa, xg), hl = _rwkv_mix(x, gpre3, layer, mu, j, shift_rows)
    mm = functools.partial(_matmul, out_dtype=BF16)
    r = mm(xr, wr, j, name="rwkv_r")
    k = mm(xk, wk, j, name="rwkv_k")
    v = mm(xv, wv, j, name="rwkv_v")
    row3 = lambda p: p.reshape(p.shape[0], 1, D)
    ld = _lora(xw, w1, w2, j, kind="decay", out_dtype=F32, bias3=row3(w0))
    a = _lora(xa, a1, a2, j, kind="iclr", out_dtype=BF16, bias3=row3(a0))
    g = _lora(xg, g1p, g2p, j, kind="gate", out_dtype=BF16)
    y, states = _rwkv(r, k, v, g, ld, a, row3(k_k), row3(k_a), r_k.reshape(-1, 1, D), row3(lnx_g), row3(lnx_b),
                      state_wkv6, j)
    out = _matmul(y, wo, j, name="rwkv_o")
    p_wkv = states[:, 0]
    s_wkv = states[:, 1:].reshape(DEC_B, RWKV_H, RWKV_N, RWKV_N)
    p_shift = hl[:, 7]
    s_shift = jnp.stack([hl[:, 11], hl[:, 15]], axis=1).reshape(DEC_B, D)
    return out, (p_wkv, p_shift, s_wkv, s_shift)


def _ret_tables():
    half = RET_DK // 2
    inv = ROPE_BASE ** (-jnp.arange(half, dtype=F32) / half)
    row = jnp.arange(TT)
    pos = jnp.where(row < TP, row, PAST_LEN + jnp.maximum(row - TP, 0) % TS)
    ang = pos.astype(F32)[:, None] * inv[None, :]
    lg = jnp.log1p(-jnp.exp2(-5.0 - jnp.arange(RET_H, dtype=F32)))
    return jnp.cos(ang), jnp.sin(ang), jnp.broadcast_to(lg[:, None, None], (RET_H, 1, LANES))


def _ret_layer(h, state_ret, wq, wk, wv, wg, wo):
    q = _matmul(h, wq, 0, name="ret_q")
    k = _matmul(h, wk, 0, name="ret_k")
    v = _matmul(h, wv, 0, out_dtype=BF16, name="ret_v")
    g = _matmul(h, wg, 0, out_dtype=BF16, name="ret_g")
    cos, sin, lg3 = _ret_tables()
    s0 = state_ret[0].reshape(NB, 2, RET_H, RET_DK, RET_DV)
    y, states = _retention(q, k, v, g, cos, sin, lg3, s0)
    out = _matmul(y, wo, 0, kblock=(0, D), name="ret_o_lo")
    out = _matmul(y, wo, 0, kblock=(1, D), acc=out, name="ret_o_hi")
    return out, (states[:, 0], states[:, 1:].reshape(DEC_B, RET_H, RET_DK, RET_DV))


def _fox_layer(h, cache_k, cache_v, cache_f, page_table, wq, wk, wv, wf, bf, wo):
    q = _matmul(h, wq, 0, name="fox_q")
    k = _matmul(h, wk, 0, name="fox_k")
    v = _matmul(h, wv, 0, name="fox_v")
    wf_pad = jnp.pad(wf[0], ((0, 0), (0, LANES - FOX_H)))
    bf_pad = jnp.pad(bf[0], (0, LANES - FOX_H)).reshape(1, LANES)
    f = _fox_logf(h, wf_pad, bf_pad)
    f3 = f.reshape(NB, TT, LANES)
    c3 = _fox_prompt_cum(f3)
    ct3 = jnp.swapaxes(c3[:, :, :FOX_H], 1, 2).reshape(NB, FOX_H, 1, TT)
    nrow = TS * FOX_H
    qs, ks, vs = (_sample_rows(z) for z in (q, k, v))
    fs = _sample_rows(f)[:, :, :FOX_H]
    new_f = jnp.pad(fs, ((0, 0), (0, PAGE - TS), (0, 0)))
    call = _fox_decode_cum(page_table, cache_f[0], new_f)
    crow = call[:, :N_PAGES].reshape(DEC_B, N_PAGES, 1, PAGE * FOX_H)
    cnew = call[:, N_PAGES, :TS]
    cq = jnp.broadcast_to(cnew.reshape(DEC_B, nrow, 1), (DEC_B, nrow, FOX_HD))
    o2 = _fox_decode_attn(page_table, qs.reshape(DEC_B, nrow, FOX_HD), cache_k[0], cache_v[0], crow, cq,
                          ks.reshape(DEC_B, nrow, FOX_HD), vs.reshape(DEC_B, nrow, FOX_HD),
                          cnew.reshape(DEC_B, 1, nrow))
    dec = o2.reshape(NB, 2 * TS, D)
    y = _fox_prompt(q, k, v, c3, ct3, dec)
    out = _matmul(y, wo, 0, name="fox_o")
    heads = lambda z, t: z.reshape(-1, t, FOX_H, FOX_HD)
    outs = (heads(_prompt_rows(k), TP), heads(_prompt_rows(v), TP), _prompt_rows(f)[:, :, :FOX_H],
            heads(ks, TS), heads(vs, TS), fs)
    return out, outs


def kernel(x_prompt, x_sample, state_rwkv_wkv, state_rwkv_shift, state_ret, cache_fox_k, cache_fox_v, cache_fox_logf, state_ffn_conv, page_table, meta_tokens, norm_mix_pre, norm_mix_post, norm_ffn_pre, norm_ffn_post, rwkv_mu, rwkv_w0, rwkv_w1, rwkv_w2, rwkv_a0, rwkv_a1, rwkv_a2, rwkv_g1, rwkv_g2, rwkv_k_k, rwkv_k_a, rwkv_r_k, rwkv_wr, rwkv_wk, rwkv_wv, rwkv_wo, rwkv_lnx_g, rwkv_lnx_b, ret_wq, ret_wk, ret_wv, ret_wg, ret_wo, fox_wq, fox_wk, fox_wv, fox_wf, fox_bf, fox_wo, ffn_wgate, ffn_wup, ffn_conv_w, ffn_conv_b, ffn_wdown):
    meta = jnp.broadcast_to(meta_tokens[None].astype(F32), (NB, N_META, D))
    x = jnp.concatenate(
        [meta, x_prompt, x_sample.reshape(NB, 2 * TS, D), jnp.zeros((NB, TT - TP - 2 * TS, D), F32)], axis=1
    ).reshape(M, D)
    row3 = lambda p: p.reshape(p.shape[0], 1, p.shape[-1])
    g_mix_pre, g_mix_post, g_ffn_pre, g_ffn_post = (row3(p) for p in (norm_mix_pre, norm_mix_post, norm_ffn_pre, norm_ffn_post))
    g1p = jnp.pad(rwkv_g1, ((0, 0), (0, 0), (0, RWKV_LG_PAD - RWKV_LG)))
    g2p = jnp.pad(rwkv_g2, ((0, 0), (0, RWKV_LG_PAD - RWKV_LG), (0, 0)))
    wkv6 = state_rwkv_wkv.reshape(-1, NB, 2, RWKV_H, RWKV_N, RWKV_N)
    conv_state = jnp.concatenate(
        [jnp.zeros((DEPTH, NB, 1, 2, D_FF), F32), state_ffn_conv.reshape(DEPTH, NB, 2, 2, D_FF)], axis=2)
    cb3 = row3(ffn_conv_b)

    p_wkv, p_shift, s_wkv, s_shift, p_conv, s_conv = [], [], [], [], [], []
    h = None
    for i in range(DEPTH):
        kind = i % 3
        j = i // 3
        if kind == 0:
            out, (pw, ps, sw, ss) = _rwkv_layer(
                x, i, j, g_mix_pre, wkv6, state_rwkv_shift, rwkv_mu, rwkv_w0, rwkv_w1, rwkv_w2, rwkv_a0, rwkv_a1,
                rwkv_a2, g1p, g2p, rwkv_k_k, rwkv_k_a, rwkv_r_k, rwkv_wr, rwkv_wk, rwkv_wv, rwkv_wo, rwkv_lnx_g,
                rwkv_lnx_b)
            p_wkv.append(pw); p_shift.append(ps); s_wkv.append(sw); s_shift.append(ss)
        elif kind == 1:
            out, (p_ret, s_ret) = _ret_layer(h, state_ret, ret_wq, ret_wk, ret_wv, ret_wg, ret_wo)
        else:
            out, fox_outs = _fox_layer(h, cache_fox_k, cache_fox_v, cache_fox_logf, page_table, fox_wq, fox_wk,
                                       fox_wv, fox_wf, fox_bf, fox_wo)
        x, h = _resnorm(x, out, g_mix_post, i, g_ffn_pre, i)
        hm, cs = _ffn_up(h, ffn_wgate, ffn_wup, ffn_conv_w, cb3, conv_state[i], i)
        f = _matmul(hm, ffn_wdown, i, tn=256, kblock=(0, D_FF // 2), name="ffn_down_lo")
        f = _matmul(hm, ffn_wdown, i, tn=256, kblock=(1, D_FF // 2), acc=f, name="ffn_down_hi")
        p_conv.append(cs[:, 0]); s_conv.append(cs[:, 1:].reshape(DEC_B, 2, D_FF))
        if i + 1 < DEPTH and (i + 1) % 3 != 0:
            x, h = _resnorm(x, f, g_ffn_post, i, g_mix_pre, i + 1)
        else:
            x, h = _resnorm(x, f, g_ffn_post, i)
    x3 = x.reshape(NB, TT, D)
    y_prompt = x3[:, N_META:TP]
    y_sample = x3[:, TP:TP + 2 * TS].reshape(DEC_B, TS, D)
    p_fk, p_fv, p_ff, s_fk, s_fv, s_ff = fox_outs
    return (y_prompt, y_sample,
            jnp.stack(p_wkv), jnp.stack(p_shift), p_ret[None],
            p_fk[None], p_fv[None], p_ff[None], jnp.stack(p_conv),
            jnp.stack(s_wkv), jnp.stack(s_shift), s_ret[None],
            s_fk[None], s_fv[None], s_ff[None], jnp.stack(s_conv))
```

```python
import functools

import jax
import jax.numpy as jnp
from jax import lax
from jax.experimental import pallas as pl
from jax.experimental.pallas import tpu as pltpu

F32 = jnp.float32
BF16 = jnp.bfloat16
HIGHEST = lax.Precision.HIGHEST

D = 4096
NB = 4
SEQ = 2048
N_META = 16
TP = SEQ + N_META
DEC_B = 8
TS = 4
TT = 2080
SEQS = ((0, TP), (TP, TS), (TP + TS, TS))
M = NB * TT
DEPTH = 4
RMS_EPS = 1e-6

RWKV_N = 64
RWKV_H = D // RWKV_N
RWKV_GN_EPS = 64e-5
RWKV_L = 64
RWKV_PAIRS = 8
RWKV_ROWS = 1088
RWKV_LG = 480
RWKV_LG_PAD = 512

RET_H = 16
RET_DK = D // RET_H
RET_DV = 2 * RET_DK
RET_L = 128
ROPE_BASE = 10000.0
PAST_LEN = 8192

FOX_H = 32
FOX_HD = D // FOX_H
FOX_BQ = 512
PAGE = 128
N_PAGES = PAST_LEN // PAGE
DCUM_PAGES = 8
DATT_PAGES = 2

D_FF = 11008
FFN_ROW_BLOCKS = 5

LANES = 128
V7X_VMEM_BYTES = 64 * 1024 * 1024
V7X_VMEM_CAP = 56 * 1024 * 1024

NT_DIMS = (((1,), (1,)), ((), ()))
TN_DIMS = (((0,), (0,)), ((), ()))


def _params(sems, vmem_bytes):
    return pltpu.CompilerParams(
        dimension_semantics=sems,
        vmem_limit_bytes=int(min(max(vmem_bytes * 5 // 4, 16 * 1024 * 1024), V7X_VMEM_CAP)),
    )


def _rms(x, g):
    return x * lax.rsqrt(jnp.mean(x * x, axis=-1, keepdims=True) + RMS_EPS) * g


ROW_TILE = 208


def _mix_body(x_ref, g_ref, mu_ref, sr_ref, o0, o1, o2, o3, o4, o5, hl_ref, carry, *, tr, nchunks):
    c = pl.program_id(1)

    @pl.when(c == 0)
    def _():
        carry[...] = jnp.zeros_like(carry)

    h = _rms(x_ref[...], g_ref[...])
    row = lax.broadcasted_iota(jnp.int32, (tr, 1), 0) + c * tr
    prev = pltpu.roll(h, 1, 0)
    prev = jnp.where(row == c * tr, carry[7:8, :], prev)
    for s, (start, _) in enumerate(SEQS):
        prev = jnp.where(row == start, sr_ref[0, s:s + 1, :], prev)
    carry[...] = h[tr - 8:, :]
    xx = prev - h
    for i, o in enumerate((o0, o1, o2, o3, o4, o5)):
        o[...] = (h + xx * mu_ref[i:i + 1, :]).astype(BF16)

    @pl.when(c == nchunks - 1)
    def _():
        hl_ref[0] = h[tr - 24:tr - 8, :]


def _rwkv_mix(x, g3, layer, mu, j, shift_rows):
    tr = ROW_TILE
    nchunks = TT // tr
    row_spec = pl.BlockSpec((tr, D), lambda b, c: (b * nchunks + c, 0))
    outs = pl.pallas_call(
        functools.partial(_mix_body, tr=tr, nchunks=nchunks),
        grid=(NB, nchunks),
        in_specs=[
            row_spec,
            pl.BlockSpec((None, 1, D), lambda b, c: (layer, 0, 0)),
            pl.BlockSpec((None, 6, D), lambda b, c: (j, 0, 0)),
            pl.BlockSpec((1, 8, D), lambda b, c: (b, 0, 0)),
        ],
        out_specs=[row_spec] * 6 + [pl.BlockSpec((1, 16, D), lambda b, c: (b, 0, 0))],
        out_shape=[jax.ShapeDtypeStruct((M, D), BF16)] * 6 + [jax.ShapeDtypeStruct((NB, 16, D), F32)],
        scratch_shapes=[pltpu.VMEM((8, D), F32)],
        compiler_params=_params(("arbitrary", "arbitrary"), 48 * 1024 * 1024),
        name="rwkv_mix",
    )(x, g3, mu, shift_rows)
    return outs[:6], outs[6]


def _resnorm_body(res_ref, y_ref, gp_ref, gn_ref, xo_ref, ho_ref):
    x = res_ref[...] + _rms(y_ref[...], gp_ref[...])
    xo_ref[...] = x
    ho_ref[...] = _rms(x, gn_ref[...]).astype(BF16)


def _res_body(res_ref, y_ref, gp_ref, xo_ref):
    xo_ref[...] = res_ref[...] + _rms(y_ref[...], gp_ref[...])


def _resnorm(res, y, gpost3, lp, gnext3=None, ln=0):
    tr = ROW_TILE
    row_spec = pl.BlockSpec((tr, D), lambda i: (i, 0))
    gspec = lambda l: pl.BlockSpec((None, 1, D), lambda i: (l, 0, 0))
    if gnext3 is None:
        return pl.pallas_call(
            _res_body, grid=(M // tr,), in_specs=[row_spec, row_spec, gspec(lp)], out_specs=row_spec,
            out_shape=jax.ShapeDtypeStruct((M, D), F32),
            compiler_params=_params(("arbitrary",), 32 * 1024 * 1024), name="res_add",
        )(res, y, gpost3), None
    return pl.pallas_call(
        _resnorm_body, grid=(M // tr,), in_specs=[row_spec, row_spec, gspec(lp), gspec(ln)],
        out_specs=[row_spec, row_spec],
        out_shape=[jax.ShapeDtypeStruct((M, D), F32), jax.ShapeDtypeStruct((M, D), BF16)],
        compiler_params=_params(("arbitrary",), 40 * 1024 * 1024), name="res_norm",
    )(res, y, gpost3, gnext3)


def _mm_body(x_ref, w_ref, o_ref):
    o_ref[...] = jnp.dot(x_ref[...], w_ref[...].astype(BF16), preferred_element_type=F32).astype(o_ref.dtype)


def _mm_acc_body(x_ref, w_ref, a_ref, o_ref):
    o_ref[...] = a_ref[...] + jnp.dot(x_ref[...], w_ref[...].astype(BF16), preferred_element_type=F32)


def _matmul(x, w, layer, *, tm=TT // 2, tn=512, out_dtype=F32, name, kblock=None, acc=None):
    m, k = x.shape
    n = w.shape[-1]
    kidx, ksz = (0, k) if kblock is None else kblock
    osz = jnp.dtype(out_dtype).itemsize
    vmem = 2 * tm * ksz * 2 + 2 * ksz * tn * 4 + ksz * tn * 2 + 2 * tm * tn * osz + tm * tn * 4 + (4 << 20)
    in_specs = [
        pl.BlockSpec((tm, ksz), lambda i, j: (i, kidx)),
        pl.BlockSpec((None, ksz, tn), lambda i, j: (layer, kidx, j)),
    ]
    out_spec = pl.BlockSpec((tm, tn), lambda i, j: (i, j))
    args = (x, w)
    body = _mm_body
    aliases = {}
    if acc is not None:
        in_specs.append(out_spec)
        args = (x, w, acc)
        body = _mm_acc_body
        aliases = {2: 0}
        vmem += 2 * tm * tn * 4
    return pl.pallas_call(
        body,
        grid=(m // tm, n // tn),
        in_specs=in_specs,
        out_specs=out_spec,
        out_shape=jax.ShapeDtypeStruct((m, n), out_dtype),
        input_output_aliases=aliases,
        compiler_params=_params(("arbitrary", "arbitrary"), vmem),
        name=name,
    )(*args)


def _lora_body(*refs, kind):
    x_ref, w1_ref, w2_ref = refs[:3]
    o_ref = refs[-1]
    t = jnp.dot(x_ref[...], w1_ref[...].astype(BF16), preferred_element_type=F32)
    if kind == "decay":
        t = jnp.tanh(t)
    elif kind == "gate":
        t = jax.nn.sigmoid(t)
    z = jnp.dot(t.astype(BF16), w2_ref[...].astype(BF16), preferred_element_type=F32)
    if kind == "decay":
        z = -jnp.exp(-jax.nn.softplus(-(refs[3][...] + z)) - 0.5)
    elif kind == "iclr":
        z = jax.nn.sigmoid(refs[3][...] + z)
    o_ref[...] = z.astype(o_ref.dtype)


def _lora(x, w1, w2, j, *, kind, out_dtype, bias3=None, tm=416):
    m, k = x.shape
    r = w1.shape[-1]
    n = w2.shape[-1]
    vmem = 2 * tm * k * 2 + 3 * (k * r + r * n) * 4 + 3 * tm * n * 4 + (4 << 20)
    in_specs = [
        pl.BlockSpec((tm, k), lambda i: (i, 0)),
        pl.BlockSpec((None, k, r), lambda i: (j, 0, 0), pipeline_mode=pl.Buffered(1)),
        pl.BlockSpec((None, r, n), lambda i: (j, 0, 0), pipeline_mode=pl.Buffered(1)),
    ]
    args = (x, w1, w2)
    if bias3 is not None:
        in_specs.append(pl.BlockSpec((None, 1, n), lambda i: (j, 0, 0)))
        args += (bias3,)
    return pl.pallas_call(
        functools.partial(_lora_body, kind=kind),
        grid=(m // tm,),
        in_specs=in_specs,
        out_specs=pl.BlockSpec((tm, n), lambda i: (i, 0)),
        out_shape=jax.ShapeDtypeStruct((m, n), out_dtype),
        compiler_params=_params(("arbitrary",), vmem),
        name="rwkv_lora_" + kind,
    )(*args)


def _ffn_up_body(x_ref, wg_ref, wu_ref, cw_ref, cb_ref, st_ref, h_ref, so_ref):
    wg = wg_ref[...].astype(BF16)
    wu = wu_ref[...].astype(BF16)
    cw = cw_ref[...]
    nblk = FFN_ROW_BLOCKS
    hr = TT // nblk
    gs, us = [], []
    for r in range(nblk):
        x = x_ref[r * hr:(r + 1) * hr, :]
        gs.append(jnp.dot(x, wg, preferred_element_type=F32))
        us.append(jnp.dot(x, wu, preferred_element_type=F32))
    for r in range(nblk):
        g = gs[r]
        base = r * hr
        row = lax.broadcasted_iota(jnp.int32, (hr, 1), 0) + base
        s1 = pltpu.roll(g, 1, 0)
        s2 = pltpu.roll(g, 2, 0)
        if r > 0:
            last = gs[r - 1][hr - 2:hr, :]
            s1 = jnp.where(row == base, last[1:2, :], s1)
            s2 = jnp.where(row == base, last[0:1, :], jnp.where(row == base + 1, last[1:2, :], s2))
        for s, (start, length) in enumerate(SEQS):
            st = st_ref[0, s]
            if base <= start < base + hr:
                s1 = jnp.where(row == start, st[1:2, :], s1)
                s2 = jnp.where(row == start, st[0:1, :], s2)
            if base <= start + 1 < base + hr:
                s2 = jnp.where(row == start + 1, st[1:2, :], s2)
            end = start + length
            if base < end <= base + hr:
                so_ref[0, s] = g[end - 2 - base:end - base, :]
        c = cb_ref[...] + cw[0:1, :] * s2 + cw[1:2, :] * s1 + cw[2:3, :] * g
        act = 0.5 * c * (1.0 + lax.erf(c * (2.0 ** -0.5)))
        h_ref[base:base + hr, :] = (act * us[r]).astype(BF16)


def _ffn_up(h, wg, wu, cw, cb3, state, layer, *, tn=256):
    nj = D_FF // tn
    vmem = TT * D * 2 + 4 * D * tn * 4 + 2 * D * tn * 2 + 2 * TT * tn * 2 + 6 * TT * tn * 4 + (4 << 20)
    return pl.pallas_call(
        _ffn_up_body,
        grid=(NB, nj),
        in_specs=[
            pl.BlockSpec((TT, D), lambda b, j: (b, 0), pipeline_mode=pl.Buffered(1)),
            pl.BlockSpec((None, D, tn), lambda b, j: (layer, 0, j)),
            pl.BlockSpec((None, D, tn), lambda b, j: (layer, 0, j)),
            pl.BlockSpec((None, 3, tn), lambda b, j: (layer, 0, j)),
            pl.BlockSpec((None, 1, tn), lambda b, j: (layer, 0, j)),
            pl.BlockSpec((1, 3, 2, tn), lambda b, j: (b, 0, 0, j)),
        ],
        out_specs=[
            pl.BlockSpec((TT, tn), lambda b, j: (b, j)),
            pl.BlockSpec((1, 3, 2, tn), lambda b, j: (b, 0, 0, j)),
        ],
        out_shape=[jax.ShapeDtypeStruct((M, D_FF), BF16), jax.ShapeDtypeStruct((NB, 3, 2, D_FF), F32)],
        compiler_params=_params(("arbitrary", "arbitrary"), vmem),
        name="ffn_up",
    )(h, wg, wu, cw, cb3, state)


def _rwkv_consts():
    n = 2 * RWKV_L
    lane = lax.broadcasted_iota(jnp.int32, (1, LANES), 1)
    ri = lax.broadcasted_iota(jnp.int32, (n, n), 0)
    ci = lax.broadcasted_iota(jnp.int32, (n, n), 1)
    same = (ri // RWKV_L) == (ci // RWKV_L)
    return dict(
        m0=lane < RWKV_N,
        strict=jnp.logical_and(same, ci < ri),
        incl=jnp.logical_and(same, ci <= ri),
        eye=(ri == ci).astype(F32),
        row=lax.broadcasted_iota(jnp.int32, (RWKV_L, 1), 0),
    )


def _rwkv_prep(r, k, v, ld, a, kk_c, ka_c, cst, active):
    L = RWKV_L
    m0 = cst["m0"]

    def seg(z):
        s0 = jnp.sum(jnp.where(m0, z, 0.0), axis=-1, keepdims=True)
        s1 = jnp.sum(jnp.where(m0, 0.0, z), axis=-1, keepdims=True)
        return jnp.where(m0, s0, s1)

    kraw = k * kk_c
    kk = kraw * lax.rsqrt(seg(kraw * kraw) + 1e-12)
    kmod = k * (1.0 + (a - 1.0) * ka_c)
    aa = -kk
    bb = kk * a
    km = kmod
    vv = v
    if active is not None:
        ld, aa, bb, km, vv = (jnp.where(active, z, 0.0) for z in (ld, aa, bb, km, vv))
    cum = ld
    shift = 1
    while shift < L:
        cum = cum + jnp.where(cst["row"] >= shift, pltpu.roll(cum, shift, 0), 0.0)
        shift *= 2
    cum_l = cum[L - 1:L, :]
    e_neg = jnp.exp(-cum)
    e_rem = jnp.exp(cum_l - cum)

    def stack(z):
        return jnp.concatenate([jnp.where(m0, z, 0.0), jnp.where(m0, 0.0, z)], axis=0)

    rt = stack(r * jnp.exp(cum))
    a2 = stack(aa * jnp.exp(cum - ld)).astype(BF16)
    return dict(
        rt=rt, a2=a2, xa=jnp.concatenate([a2, rt.astype(BF16)], axis=0),
        b2=stack(bb * e_neg).astype(BF16), k2=stack(km * e_neg).astype(BF16),
        bh2=stack(bb * e_rem).astype(BF16), kh2=stack(km * e_rem).astype(BF16),
        v2=stack(vv).astype(BF16), decay=jnp.exp(cum_l), kmod=kmod)


def _rwkv_chunks(preps, ws, cst):
    L = RWKV_L
    n = 2 * L
    dot = lambda p, q: jnp.dot(p, q, preferred_element_type=F32)
    nt = lambda p, q: lax.dot_general(p, q, NT_DIMS, preferred_element_type=F32)
    tn = lambda p, q: lax.dot_general(p, q, TN_DIMS, preferred_element_type=F32)
    xb = [nt(p["xa"], p["b2"]) for p in preps]
    xk = [nt(p["xa"], p["k2"]) for p in preps]
    a_ab = [jnp.where(cst["strict"], z[:n], 0.0) for z in xb]
    a_rb = [jnp.where(cst["incl"], z[n:], 0.0).astype(BF16) for z in xb]
    a_ak = [jnp.where(cst["strict"], z[:n], 0.0).astype(BF16) for z in xk]
    a_rk = [jnp.where(cst["incl"], z[n:], 0.0).astype(BF16) for z in xk]
    akv = [dot(m, p["v2"]) for m, p in zip(a_ak, preps)]
    tinv = [cst["eye"] + z for z in a_ab]
    q = a_ab
    for _ in range(max(L.bit_length() - 2, 0)):
        qb = [z.astype(BF16) for z in q]
        q = [dot(z, z) for z in qb]
        tinv = [t + dot(t.astype(BF16), z.astype(BF16)) for t, z in zip(tinv, q)]
    tu = [dot(t.astype(BF16), jnp.concatenate([u.astype(BF16), p["a2"]], axis=1)).astype(BF16)
          for t, u, p in zip(tinv, akv, preps)]
    zero = jnp.zeros((n, LANES), BF16)
    uv = [jnp.concatenate([z, jnp.concatenate([p["v2"], zero], axis=1)], axis=0) for z, p in zip(tu, preps)]
    ru = [dot(jnp.concatenate([m1, m2], axis=1), z) for m1, m2, z in zip(a_rb, a_rk, uv)]
    bu = [tn(jnp.concatenate([p["bh2"], p["kh2"]], axis=0), z) for p, z in zip(preps, uv)]
    wh = [w.astype(BF16) for w in ws]
    o2 = [dot((p["rt"] + z[:, LANES:]).astype(BF16), w) + z[:, :LANES] for p, z, w in zip(preps, ru, wh)]
    phi = [cst["eye"] * p["decay"] + z[:, LANES:] for p, z in zip(preps, bu)]
    ph = [z.astype(BF16) for z in phi]
    pl_ = [(z - zh.astype(F32)).astype(BF16) for z, zh in zip(phi, ph)]
    wl = [(w - zh.astype(F32)).astype(BF16) for w, zh in zip(ws, wh)]
    hh = [dot(jnp.concatenate([x, y], axis=1), jnp.concatenate([z, z], axis=0)) for x, y, z in zip(ph, pl_, wh)]
    hl = [dot(x, y) for x, y in zip(ph, wl)]
    w_new = [(x + y) + b[:, :LANES] for x, y, b in zip(hh, hl, bu)]
    return [(z[:L] + z[L:], w) for z, w in zip(o2, w_new)]


def _rwkv_body(r_ref, k_ref, v_ref, g_ref, ld_ref, a_ref, kk_ref, ka_ref, rk_ref, lg_ref, lb_ref, s0_ref,
               y_ref, so_ref, w_s, *, npairs):
    L = RWKV_L
    nfull = TP // L
    blk = pl.program_id(2)
    cst = _rwkv_consts()
    m0 = cst["m0"]

    def seg(z):
        s0 = jnp.sum(jnp.where(m0, z, 0.0), axis=-1, keepdims=True)
        s1 = jnp.sum(jnp.where(m0, 0.0, z), axis=-1, keepdims=True)
        return jnp.where(m0, s0, s1)

    def prep_all(rows, active):
        preps = []
        for p in range(npairs):
            ln = slice(LANES * p, LANES * (p + 1))
            preps.append(_rwkv_prep(r_ref[0, rows, ln].astype(F32), k_ref[0, rows, ln].astype(F32),
                                    v_ref[0, rows, ln].astype(F32), ld_ref[0, rows, ln],
                                    a_ref[0, rows, ln].astype(F32), kk_ref[:, ln], ka_ref[:, ln], cst, active))
        return preps

    def finish(rows, preps, outs):
        ys = []
        for p, (prep, (o, _)) in enumerate(zip(preps, outs)):
            ln = slice(LANES * p, LANES * (p + 1))
            mean = seg(o) * (1.0 / RWKV_N)
            xc = o - mean
            var = seg(xc * xc) * (1.0 / RWKV_N)
            on = xc * lax.rsqrt(var + RWKV_GN_EPS) * lg_ref[:, ln] + lb_ref[:, ln]
            bonus = seg(r_ref[0, rows, ln].astype(F32) * prep["kmod"] * rk_ref[:, ln]) * v_ref[0, rows, ln].astype(F32)
            ys.append((on + bonus) * g_ref[0, rows, ln].astype(F32))
        return ys

    @pl.when(blk == 0)
    def _():
        w_s[...] = jnp.zeros_like(w_s)

    def body(c, ws):
        rows = pl.ds(pl.multiple_of(c * L, L), L)
        preps = prep_all(rows, None)
        outs = _rwkv_chunks(preps, ws, cst)
        for p, y in enumerate(finish(rows, preps, outs)):
            y_ref[0, rows, LANES * p:LANES * (p + 1)] = y.astype(BF16)
        return tuple(w for _, w in outs)

    per_blk = RWKV_ROWS // L
    nchunks = jnp.where(blk == 0, per_blk, nfull - per_blk)
    ws = lax.fori_loop(0, nchunks, body, tuple(w_s[p] for p in range(npairs)))
    for p in range(npairs):
        w_s[p] = ws[p]

    @pl.when(blk == 1)
    def _():
        base = RWKV_ROWS
        win = slice(TT - L - base, TT - base)
        lrow = lax.broadcasted_iota(jnp.int32, (L, 1), 0) + (TT - L)
        zero = jnp.zeros((RWKV_N, RWKV_N), F32)
        ytail = [jnp.zeros((L, LANES), F32) for _ in range(npairs)]
        for s, (start, length) in enumerate(SEQS):
            active = jnp.logical_and(lrow >= max(start, nfull * L), lrow < start + length)
            w0s = []
            for p in range(npairs):
                if s == 0:
                    w0s.append(ws[p])
                else:
                    s_in = s0_ref[0, s - 1, 2 * p:2 * p + 2]
                    st = jnp.concatenate([jnp.concatenate([s_in[0], zero], axis=1),
                                          jnp.concatenate([zero, s_in[1]], axis=1)], axis=0)
                    w0s.append(st.T)
            preps = prep_all(win, active)
            outs = _rwkv_chunks(preps, w0s, cst)
            for p, (y, (_, w_new)) in enumerate(zip(finish(win, preps, outs), outs)):
                ytail[p] = jnp.where(active, y, ytail[p])
                wt = w_new.T
                so_ref[0, s, 2 * p] = wt[:RWKV_N, :RWKV_N]
                so_ref[0, s, 2 * p + 1] = wt[RWKV_N:, RWKV_N:]
        for p in range(npairs):
            y_ref[0, nfull * L - base:TT - base, LANES * p:LANES * (p + 1)] = \
                ytail[p][nfull * L - (TT - L):].astype(BF16)


def _rwkv(r, k, v, g, ld, a, kk3, ka3, rk3, lg3, lb3, s0, j, *, npairs=RWKV_PAIRS):
    pw = LANES * npairs
    nh = 2 * npairs
    act = lambda: pl.BlockSpec((1, RWKV_ROWS, pw), lambda b, p, t: (b, t, p))
    par = lambda: pl.BlockSpec((None, 1, pw), lambda b, p, t: (j, 0, p))
    vmem = 2 * RWKV_ROWS * pw * (5 * 2 + 4) + 2 * RWKV_ROWS * pw * 2 + (20 << 20)
    return pl.pallas_call(
        functools.partial(_rwkv_body, npairs=npairs),
        grid=(NB, RWKV_H // nh, 2),
        in_specs=[act(), act(), act(), act(), act(), act(), par(), par(), par(), par(), par(),
                  pl.BlockSpec((None, 1, 2, nh, RWKV_N, RWKV_N), lambda b, p, t: (j, b, 0, p, 0, 0))],
        out_specs=[act(), pl.BlockSpec((1, 3, nh, RWKV_N, RWKV_N), lambda b, p, t: (b, 0, p, 0, 0))],
        out_shape=[jax.ShapeDtypeStruct((NB, TT, D), BF16),
                   jax.ShapeDtypeStruct((NB, 3, RWKV_H, RWKV_N, RWKV_N), F32)],
        scratch_shapes=[pltpu.VMEM((npairs, LANES, LANES), F32)],
        compiler_params=_params(("arbitrary", "arbitrary", "arbitrary"), vmem),
        name="rwkv7",
    )(r, k, v, g, ld, a, kk3, ka3, rk3, lg3, lb3, s0)


def _ret_body(q_ref, k_ref, v_ref, g_ref, cos_ref, sin_ref, lg_ref, s0_ref, y_ref, so_ref):
    L = RET_L
    nfull = TP // L
    half = RET_DK // 2
    lg = lg_ref[:, :1]
    ii = lax.broadcasted_iota(jnp.int32, (L, L), 0)
    jj = lax.broadcasted_iota(jnp.int32, (L, L), 1)
    ri = lax.broadcasted_iota(jnp.int32, (L, 1), 0)
    dfull = jnp.where(jj <= ii, jnp.exp((ii - jj).astype(F32) * lg), 0.0)

    def rot(z, cs, sn):
        z1 = z[:, :half]
        z2 = z[:, half:]
        return jnp.concatenate([z1 * cs - z2 * sn, z1 * sn + z2 * cs], axis=1)

    def chunk(rows, s_in, off, length):
        cs = cos_ref[rows, :]
        sn = sin_ref[rows, :]
        q = rot(q_ref[rows, :], cs, sn).astype(BF16)
        k = rot(k_ref[rows, :], cs, sn) * (RET_DK ** -0.5)
        v = v_ref[rows, :]
        nloc = (ri - off).astype(F32)
        act = jnp.logical_and(ri >= off, ri < off + length)
        if off == 0 and length == L:
            dmat = dfull
        else:
            dmat = jnp.where(jnp.logical_and(jj >= off, ii < off + length), dfull, 0.0)
        inner = jnp.exp((nloc + 1.0) * lg)
        kdec = jnp.where(act, jnp.exp((length - 1.0 - nloc) * lg), 0.0)
        sc = lax.dot_general(q, k.astype(BF16), NT_DIMS, preferred_element_type=F32) * dmat
        o = jnp.dot(sc.astype(BF16), v, preferred_element_type=F32)
        o = o + jnp.dot(q, s_in.astype(BF16), preferred_element_type=F32) * inner
        s_out = jnp.exp(length * lg) * s_in + lax.dot_general((k * kdec).astype(BF16), v, TN_DIMS,
                                                                preferred_element_type=F32)
        o = o * lax.rsqrt(jnp.mean(o * o, axis=-1, keepdims=True) + RMS_EPS)
        gate = g_ref[rows, :].astype(F32)
        return gate * jax.nn.sigmoid(gate) * o, s_out, act

    so_ref[0, 0, 0] = jnp.zeros((RET_DK, RET_DV), F32)

    def body(c, carry):
        rows = pl.ds(pl.multiple_of(c * L, L), L)
        y, s_out, _ = chunk(rows, so_ref[0, 0, 0], 0, L)
        y_ref[rows, :] = y.astype(BF16)
        so_ref[0, 0, 0] = s_out
        return carry

    lax.fori_loop(0, nfull, body, 0)

    win = slice(TT - L, TT)
    ytail = jnp.zeros((L, RET_DV), F32)
    for s, (start, length) in enumerate(SEQS):
        lo = max(start, nfull * L)
        s_in = so_ref[0, 0, 0] if s == 0 else s0_ref[0, s - 1, 0]
        y, s_out, act = chunk(win, s_in, lo - (TT - L), start + length - lo)
        ytail = jnp.where(act, y, ytail)
        so_ref[0, s, 0] = s_out
    y_ref[nfull * L:TT, :] = ytail[nfull * L - (TT - L):].astype(BF16)


def _retention(q, k, v, g, cos, sin, lg3, s0):
    spec = lambda w: pl.BlockSpec((TT, w), lambda b, h: (b, h))
    tab = pl.BlockSpec((TT, RET_DK // 2), lambda b, h: (0, 0))
    vmem = 2 * TT * (2 * RET_DK * 4 + 2 * RET_DV * 2 + RET_DV * 2 + 2 * 128 * 4) + 12 * RET_DK * RET_DV * 4 + (8 << 20)
    return pl.pallas_call(
        _ret_body,
        grid=(NB, RET_H),
        in_specs=[spec(RET_DK), spec(RET_DK), spec(RET_DV), spec(RET_DV), tab, tab,
                  pl.BlockSpec((None, 1, LANES), lambda b, h: (h, 0, 0)),
                  pl.BlockSpec((1, 2, 1, RET_DK, RET_DV), lambda b, h: (b, 0, h, 0, 0))],
        out_specs=[spec(RET_DV), pl.BlockSpec((1, 3, 1, RET_DK, RET_DV), lambda b, h: (b, 0, h, 0, 0))],
        out_shape=[jax.ShapeDtypeStruct((M, RET_H * RET_DV), BF16),
                   jax.ShapeDtypeStruct((NB, 3, RET_H, RET_DK, RET_DV), F32)],
        compiler_params=_params(("arbitrary", "arbitrary"), vmem),
        name="retention",
    )(q, k, v, g, cos, sin, lg3, s0)


def _fproj_body(x_ref, w_ref, b_ref, o_ref):
    z = jnp.dot(x_ref[...], w_ref[...].astype(BF16), preferred_element_type=F32) + b_ref[...]
    o_ref[...] = jax.nn.log_sigmoid(z)


def _fox_logf(h, wf_pad, bf_pad, *, tm=416):
    return pl.pallas_call(
        _fproj_body, grid=(M // tm,),
        in_specs=[pl.BlockSpec((tm, D), lambda i: (i, 0)), pl.BlockSpec((D, LANES), lambda i: (0, 0)),
                  pl.BlockSpec((1, LANES), lambda i: (0, 0))],
        out_specs=pl.BlockSpec((tm, LANES), lambda i: (i, 0)),
        out_shape=jax.ShapeDtypeStruct((M, LANES), F32),
        compiler_params=_params(("arbitrary",), 24 * 1024 * 1024), name="fox_logf",
    )(h, wf_pad, bf_pad)


def _tri(n):
    return (lax.broadcasted_iota(jnp.int32, (n, n), 1) <= lax.broadcasted_iota(jnp.int32, (n, n), 0)).astype(F32)


def _pcum_body(f_ref, c_ref):
    n = LANES
    tri = _tri(n)
    carry = jnp.zeros((1, LANES), F32)
    nfull = TP // n
    for blk in range(nfull):
        cb = jnp.dot(tri, f_ref[0, blk * n:(blk + 1) * n, :], precision=HIGHEST, preferred_element_type=F32) + carry
        c_ref[0, blk * n:(blk + 1) * n, :] = cb
        carry = cb[n - 1:n, :]
    row = lax.broadcasted_iota(jnp.int32, (n, 1), 0) + (TT - n)
    x = jnp.where(row >= nfull * n, f_ref[0, TT - n:TT, :], 0.0)
    cb = jnp.dot(tri, x, precision=HIGHEST, preferred_element_type=F32) + carry
    c_ref[0, nfull * n:TT, :] = cb[nfull * n - (TT - n):, :]


def _fox_prompt_cum(f3):
    spec = pl.BlockSpec((1, TT, LANES), lambda b: (b, 0, 0))
    return pl.pallas_call(
        _pcum_body, grid=(NB,), in_specs=[spec], out_specs=spec,
        out_shape=jax.ShapeDtypeStruct((NB, TT, LANES), F32),
        compiler_params=_params(("arbitrary",), 16 * 1024 * 1024), name="fox_cum",
    )(f3)


def _softmax_step(s, vb, m, l, acc):
    m_new = jnp.maximum(m, jnp.max(s, axis=-1, keepdims=True))
    alpha = jnp.exp(m - m_new)
    p = jnp.exp(s - m_new)
    l = alpha * l + jnp.sum(p, axis=-1, keepdims=True)
    acc = alpha * acc + jnp.dot(p.astype(BF16), vb, preferred_element_type=F32)
    return m_new, l, acc


def _fox_body(q_ref, k_ref, v_ref, c_ref, ct_ref, dec_ref, y_ref):
    h = pl.program_id(1)
    bq = FOX_BQ
    nq = TP // bq
    scale = FOX_HD ** -0.5
    pick = (lax.broadcasted_iota(jnp.int32, (LANES, LANES), 0) == h).astype(F32)

    def attend(qrows, nrows, kv_blocks, finish):
        q = (q_ref[qrows, :] * scale).astype(BF16)
        cq = jnp.dot(c_ref[0, qrows, :], pick, precision=HIGHEST, preferred_element_type=F32)[:, :1]

        def scores(krows, mask):
            s = lax.dot_general(q, k_ref[krows, :].astype(BF16), NT_DIMS, preferred_element_type=F32)
            s = s + cq - ct_ref[:, krows]
            return s if mask is None else jnp.where(mask, s, -jnp.inf)

        m = jnp.full((nrows, 1), -jnp.inf, F32)
        l = jnp.zeros((nrows, 1), F32)
        acc = jnp.zeros((nrows, FOX_HD), F32)
        s_next = scores(*kv_blocks[0])
        for idx, (krows, _) in enumerate(kv_blocks):
            s = s_next
            if idx + 1 < len(kv_blocks):
                s_next = scores(*kv_blocks[idx + 1])
            m, l, acc = _softmax_step(s, v_ref[krows, :].astype(BF16), m, l, acc)
            yield
        finish(acc / l)

    rr = lax.broadcasted_iota(jnp.int32, (bq, bq), 0)
    cc = lax.broadcasted_iota(jnp.int32, (bq, bq), 1)
    causal = cc <= rr
    blocks = [slice(i * bq, (i + 1) * bq) for i in range(nq)]

    def store(rows):
        def finish(o):
            y_ref[rows, :] = o.astype(BF16)
        return finish

    streams = [attend(blocks[qi], bq, [(blocks[kj], causal if kj == qi else None) for kj in range(qi + 1)],
                      store(blocks[qi])) for qi in range(nq)]
    nt = TT - nq * bq
    npr = TP - nq * bq
    qrow = lax.broadcasted_iota(jnp.int32, (nt, LANES), 0) + nq * bq
    kcol = lax.broadcasted_iota(jnp.int32, (nt, LANES), 1) + (TT - LANES)
    tail_mask = jnp.logical_and(kcol >= nq * bq, kcol <= qrow)

    def finish_tail(o):
        y_ref[nq * bq:TT, :] = jnp.concatenate(
            [o[:npr], dec_ref[0], jnp.zeros((TT - TP - 2 * TS, FOX_HD), F32)], axis=0).astype(BF16)

    streams.append(attend(slice(nq * bq, TT), nt, [(b, None) for b in blocks] + [(slice(TT - LANES, TT), tail_mask)],
                          finish_tail))
    while streams:
        streams = [g for g in streams if next(g, True) is None]


def _fox_prompt(q, k, v, c3, ct3, dec):
    spec = pl.BlockSpec((TT, FOX_HD), lambda b, h: (b, h))
    return pl.pallas_call(
        _fox_body,
        grid=(NB, FOX_H),
        in_specs=[spec, spec, spec,
                  pl.BlockSpec((1, TT, LANES), lambda b, h: (b, 0, 0)),
                  pl.BlockSpec((None, None, 1, TT), lambda b, h: (b, h, 0, 0)),
                  pl.BlockSpec((1, 2 * TS, FOX_HD), lambda b, h: (b, 0, h))],
        out_specs=spec,
        out_shape=jax.ShapeDtypeStruct((M, D), BF16),
        compiler_params=_params(("arbitrary", "arbitrary"), 40 * 1024 * 1024),
        name="fox_prompt",
    )(q, k, v, c3, ct3, dec)


def _dcum_body(pt_ref, *refs):
    del pt_ref
    page_refs = refs[:DCUM_PAGES]
    nf_ref, o_ref, carry = refs[DCUM_PAGES:]
    p = pl.program_id(1)
    tri = _tri(PAGE)

    @pl.when(p == 0)
    def _():
        carry[...] = jnp.zeros_like(carry)

    @pl.when(p < N_PAGES // DCUM_PAGES)
    def _():
        c = carry[...]
        for i, ref in enumerate(page_refs):
            cb = jnp.dot(tri, ref[0], precision=HIGHEST, preferred_element_type=F32) + c
            o_ref[0, i] = cb
            c = cb[PAGE - 1:PAGE, :]
        carry[...] = c

    @pl.when(p == N_PAGES // DCUM_PAGES)
    def _():
        cb = jnp.dot(tri, nf_ref[0], precision=HIGHEST, preferred_element_type=F32) + carry[...]
        for i in range(DCUM_PAGES):
            o_ref[0, i] = cb


def _fox_decode_cum(page_table, cache_f, new_f):
    nsteps = N_PAGES // DCUM_PAGES

    def page_spec(i):
        return pl.BlockSpec((1, PAGE, FOX_H),
                            lambda b, p, pt: (pt[b, jnp.minimum(p, nsteps - 1) * DCUM_PAGES + i], 0, 0))

    grid_spec = pltpu.PrefetchScalarGridSpec(
        num_scalar_prefetch=1,
        grid=(DEC_B, nsteps + 1),
        in_specs=[page_spec(i) for i in range(DCUM_PAGES)]
        + [pl.BlockSpec((1, PAGE, FOX_H), lambda b, p, pt: (b, 0, 0))],
        out_specs=pl.BlockSpec((1, DCUM_PAGES, PAGE, FOX_H), lambda b, p, pt: (b, p, 0, 0)),
        scratch_shapes=[pltpu.VMEM((1, FOX_H), F32)],
    )
    return pl.pallas_call(
        _dcum_body, grid_spec=grid_spec,
        out_shape=jax.ShapeDtypeStruct((DEC_B, N_PAGES + DCUM_PAGES, PAGE, FOX_H), F32),
        compiler_params=_params(("arbitrary", "arbitrary"), 16 * 1024 * 1024), name="fox_decode_cum",
    )(page_table, *([cache_f] * DCUM_PAGES), new_f)


def _dattn_body(pt_ref, q_ref, *refs):
    del pt_ref
    k_refs = refs[:DATT_PAGES]
    v_refs = refs[DATT_PAGES:2 * DATT_PAGES]
    crow_ref, cq_ref, kn_ref, vn_ref, cn_ref, o_ref, m_s, l_s, acc_s = refs[2 * DATT_PAGES:]
    p = pl.program_id(1)
    nrow = TS * FOX_H
    nsteps = N_PAGES // DATT_PAGES

    @pl.when(p == 0)
    def _():
        m_s[...] = jnp.full_like(m_s, -jnp.inf)
        l_s[...] = jnp.zeros_like(l_s)
        acc_s[...] = jnp.zeros_like(acc_s)

    q = (q_ref[0] * (FOX_HD ** -0.5)).astype(BF16)
    cq = cq_ref[0][:, :1]
    hrow = lax.broadcasted_iota(jnp.int32, (nrow, 1), 0) % FOX_H

    def update(ss, vbs):
        m_prev = m_s[...]
        m_new = m_prev
        for s in ss:
            m_new = jnp.maximum(m_new, jnp.max(s, axis=-1, keepdims=True))
        alpha = jnp.exp(m_prev - m_new)
        l = alpha * l_s[...]
        acc = alpha * acc_s[...]
        for s, vb in zip(ss, vbs):
            pr = jnp.exp(s - m_new)
            l = l + jnp.sum(pr, axis=-1, keepdims=True)
            acc = acc + jnp.dot(pr.astype(BF16), vb, preferred_element_type=F32)
        m_s[...] = m_new
        l_s[...] = l
        acc_s[...] = acc

    @pl.when(p < nsteps)
    def _():
        hcol = lax.broadcasted_iota(jnp.int32, (1, PAGE * FOX_H), 1) % FOX_H
        same = hcol == hrow
        ss = []
        for i, k_ref in enumerate(k_refs):
            kb = k_ref[0].reshape(PAGE * FOX_H, FOX_HD).astype(BF16)
            s = lax.dot_general(q, kb, NT_DIMS, preferred_element_type=F32) + cq - crow_ref[0, i]
            ss.append(jnp.where(same, s, -jnp.inf))
        update(ss, [v_ref[0].reshape(PAGE * FOX_H, FOX_HD).astype(BF16) for v_ref in v_refs])

    @pl.when(p == nsteps)
    def _():
        s = lax.dot_general(q, kn_ref[0].astype(BF16), NT_DIMS, preferred_element_type=F32) + cq - cn_ref[0]
        col = lax.broadcasted_iota(jnp.int32, (1, nrow), 1)
        row = lax.broadcasted_iota(jnp.int32, (nrow, 1), 0)
        valid = jnp.logical_and(col % FOX_H == hrow, col // FOX_H <= row // FOX_H)
        update([jnp.where(valid, s, -jnp.inf)], [vn_ref[0].astype(BF16)])
        o_ref[0] = acc_s[...] / l_s[...]


def _fox_decode_attn(page_table, q2, cache_k, cache_v, crow, cq, kn, vn, cn):
    nrow = TS * FOX_H
    nsteps = N_PAGES // DATT_PAGES

    def page_spec(i):
        return pl.BlockSpec((1, PAGE, FOX_H, FOX_HD),
                            lambda b, p, pt: (pt[b, jnp.minimum(p, nsteps - 1) * DATT_PAGES + i], 0, 0, 0))

    per_b = pl.BlockSpec((1, nrow, FOX_HD), lambda b, p, pt: (b, 0, 0))
    grid_spec = pltpu.PrefetchScalarGridSpec(
        num_scalar_prefetch=1,
        grid=(DEC_B, nsteps + 1),
        in_specs=[per_b]
        + [page_spec(i) for i in range(DATT_PAGES)] * 2
        + [pl.BlockSpec((1, DATT_PAGES, 1, PAGE * FOX_H), lambda b, p, pt: (b, jnp.minimum(p, nsteps - 1), 0, 0)),
           per_b, per_b, per_b,
           pl.BlockSpec((1, 1, nrow), lambda b, p, pt: (b, 0, 0))],
        out_specs=per_b,
        scratch_shapes=[pltpu.VMEM((nrow, 1), F32), pltpu.VMEM((nrow, 1), F32), pltpu.VMEM((nrow, FOX_HD), F32)],
    )
    return pl.pallas_call(
        _dattn_body, grid_spec=grid_spec,
        out_shape=jax.ShapeDtypeStruct((DEC_B, nrow, FOX_HD), F32),
        compiler_params=_params(("arbitrary", "arbitrary"), 44 * 1024 * 1024), name="fox_decode_attn",
    )(page_table, q2, *([cache_k] * DATT_PAGES), *([cache_v] * DATT_PAGES), crow, cq, kn, vn, cn)


def _sample_rows(a):
    n = a.shape[-1]
    return a.reshape(NB, TT, n)[:, TP:TP + 2 * TS].reshape(DEC_B, TS, n)


def _prompt_rows(a):
    n = a.shape[-1]
    return a.reshape(NB, TT, n)[:, :TP]


def _pack_states(prompt_like_zero, sample):
    s = sample.reshape((NB, 2) + sample.shape[1:])
    return jnp.concatenate([jnp.zeros_like(s[:, :1]) if prompt_like_zero else s[:, :0], s], axis=1)


def _rwkv_layer(x, layer, j, gpre3, state_wkv6, state_shift, mu, w0, w1, w2, a0, a1, a2, g1p, g2p, k_k, k_a, r_k,
                wr, wk, wv, wo, lnx_g, lnx_b):
    shift_rows = jnp.concatenate(
        [jnp.zeros((NB, 1, D), F32), state_shift[j].reshape(NB, 2, D), jnp.zeros((NB, 5, D), F32)], axis=1)
    (xr, xw, xk, xv, xa, xg), hl = _rwkv_mix(x, gpre3, layer, mu, j, shift_rows)
    mm = functools.partial(_matmul, out_dtype=BF16)
    r = mm(xr, wr, j, name="rwkv_r")
    k = mm(xk, wk, j, name="rwkv_k")
    v = mm(xv, wv, j, name="rwkv_v")
    row3 = lambda p: p.reshape(p.shape[0], 1, D)
    ld = _lora(xw, w1, w2, j, kind="decay", out_dtype=F32, bias3=row3(w0))
    a = _lora(xa, a1, a2, j, kind="iclr", out_dtype=BF16, bias3=row3(a0))
    g = _lora(xg, g1p, g2p, j, kind="gate", out_dtype=BF16)
    t3 = lambda z: z.reshape(NB, TT, D)
    y, states = _rwkv(t3(r), t3(k), t3(v), t3(g), t3(ld), t3(a), row3(k_k), row3(k_a), r_k.reshape(-1, 1, D),
                      row3(lnx_g), row3(lnx_b), state_wkv6, j)
    out = _matmul(y.reshape(M, D), wo, j, name="rwkv_o")
    p_wkv = states[:, 0]
    s_wkv = states[:, 1:].reshape(DEC_B, RWKV_H, RWKV_N, RWKV_N)
    p_shift = hl[:, 7]
    s_shift = jnp.stack([hl[:, 11], hl[:, 15]], axis=1).reshape(DEC_B, D)
    return out, (p_wkv, p_shift, s_wkv, s_shift)


def _ret_tables():
    half = RET_DK // 2
    inv = ROPE_BASE ** (-jnp.arange(half, dtype=F32) / half)
    row = jnp.arange(TT)
    pos = jnp.where(row < TP, row, PAST_LEN + jnp.maximum(row - TP, 0) % TS)
    ang = pos.astype(F32)[:, None] * inv[None, :]
    lg = jnp.log1p(-jnp.exp2(-5.0 - jnp.arange(RET_H, dtype=F32)))
    return jnp.cos(ang), jnp.sin(ang), jnp.broadcast_to(lg[:, None, None], (RET_H, 1, LANES))


def _ret_layer(h, state_ret, wq, wk, wv, wg, wo):
    q = _matmul(h, wq, 0, name="ret_q")
    k = _matmul(h, wk, 0, name="ret_k")
    v = _matmul(h, wv, 0, out_dtype=BF16, name="ret_v")
    g = _matmul(h, wg, 0, out_dtype=BF16, name="ret_g")
    cos, sin, lg3 = _ret_tables()
    s0 = state_ret[0].reshape(NB, 2, RET_H, RET_DK, RET_DV)
    y, states = _retention(q, k, v, g, cos, sin, lg3, s0)
    out = _matmul(y, wo, 0, kblock=(0, D), name="ret_o_lo")
    out = _matmul(y, wo, 0, kblock=(1, D), acc=out, name="ret_o_hi")
    return out, (states[:, 0], states[:, 1:].reshape(DEC_B, RET_H, RET_DK, RET_DV))


def _fox_layer(h, cache_k, cache_v, cache_f, page_table, wq, wk, wv, wf, bf, wo):
    q = _matmul(h, wq, 0, name="fox_q")
    k = _matmul(h, wk, 0, name="fox_k")
    v = _matmul(h, wv, 0, name="fox_v")
    wf_pad = jnp.pad(wf[0], ((0, 0), (0, LANES - FOX_H)))
    bf_pad = jnp.pad(bf[0], (0, LANES - FOX_H)).reshape(1, LANES)
    f = _fox_logf(h, wf_pad, bf_pad)
    f3 = f.reshape(NB, TT, LANES)
    c3 = _fox_prompt_cum(f3)
    ct3 = jnp.swapaxes(c3[:, :, :FOX_H], 1, 2).reshape(NB, FOX_H, 1, TT)
    nrow = TS * FOX_H
    qs, ks, vs = (_sample_rows(z) for z in (q, k, v))
    fs = _sample_rows(f)[:, :, :FOX_H]
    new_f = jnp.pad(fs, ((0, 0), (0, PAGE - TS), (0, 0)))
    call = _fox_decode_cum(page_table, cache_f[0], new_f)
    crow = call[:, :N_PAGES].reshape(DEC_B, N_PAGES, 1, PAGE * FOX_H)
    cnew = call[:, N_PAGES, :TS]
    cq = jnp.broadcast_to(cnew.reshape(DEC_B, nrow, 1), (DEC_B, nrow, FOX_HD))
    o2 = _fox_decode_attn(page_table, qs.reshape(DEC_B, nrow, FOX_HD), cache_k[0], cache_v[0], crow, cq,
                          ks.reshape(DEC_B, nrow, FOX_HD), vs.reshape(DEC_B, nrow, FOX_HD),
                          cnew.reshape(DEC_B, 1, nrow))
    dec = o2.reshape(NB, 2 * TS, D)
    y = _fox_prompt(q, k, v, c3, ct3, dec)
    out = _matmul(y, wo, 0, name="fox_o")
    heads = lambda z, t: z.reshape(-1, t, FOX_H, FOX_HD)
    outs = (heads(_prompt_rows(k), TP), heads(_prompt_rows(v), TP), _prompt_rows(f)[:, :, :FOX_H],
            heads(ks, TS), heads(vs, TS), fs)
    return out, outs


def kernel(x_prompt, x_sample, state_rwkv_wkv, state_rwkv_shift, state_ret, cache_fox_k, cache_fox_v, cache_fox_logf, state_ffn_conv, page_table, meta_tokens, norm_mix_pre, norm_mix_post, norm_ffn_pre, norm_ffn_post, rwkv_mu, rwkv_w0, rwkv_w1, rwkv_w2, rwkv_a0, rwkv_a1, rwkv_a2, rwkv_g1, rwkv_g2, rwkv_k_k, rwkv_k_a, rwkv_r_k, rwkv_wr, rwkv_wk, rwkv_wv, rwkv_wo, rwkv_lnx_g, rwkv_lnx_b, ret_wq, ret_wk, ret_wv, ret_wg, ret_wo, fox_wq, fox_wk, fox_wv, fox_wf, fox_bf, fox_wo, ffn_wgate, ffn_wup, ffn_conv_w, ffn_conv_b, ffn_wdown):
    meta = jnp.broadcast_to(meta_tokens[None].astype(F32), (NB, N_META, D))
    x = jnp.concatenate(
        [meta, x_prompt, x_sample.reshape(NB, 2 * TS, D), jnp.zeros((NB, TT - TP - 2 * TS, D), F32)], axis=1
    ).reshape(M, D)
    row3 = lambda p: p.reshape(p.shape[0], 1, p.shape[-1])
    g_mix_pre, g_mix_post, g_ffn_pre, g_ffn_post = (row3(p) for p in (norm_mix_pre, norm_mix_post, norm_ffn_pre, norm_ffn_post))
    g1p = jnp.pad(rwkv_g1, ((0, 0), (0, 0), (0, RWKV_LG_PAD - RWKV_LG)))
    g2p = jnp.pad(rwkv_g2, ((0, 0), (0, RWKV_LG_PAD - RWKV_LG), (0, 0)))
    wkv6 = state_rwkv_wkv.reshape(-1, NB, 2, RWKV_H, RWKV_N, RWKV_N)
    conv_state = jnp.concatenate(
        [jnp.zeros((DEPTH, NB, 1, 2, D_FF), F32), state_ffn_conv.reshape(DEPTH, NB, 2, 2, D_FF)], axis=2)
    cb3 = row3(ffn_conv_b)

    p_wkv, p_shift, s_wkv, s_shift, p_conv, s_conv = [], [], [], [], [], []
    h = None
    for i in range(DEPTH):
        kind = i % 3
        j = i // 3
        if kind == 0:
            out, (pw, ps, sw, ss) = _rwkv_layer(
                x, i, j, g_mix_pre, wkv6, state_rwkv_shift, rwkv_mu, rwkv_w0, rwkv_w1, rwkv_w2, rwkv_a0, rwkv_a1,
                rwkv_a2, g1p, g2p, rwkv_k_k, rwkv_k_a, rwkv_r_k, rwkv_wr, rwkv_wk, rwkv_wv, rwkv_wo, rwkv_lnx_g,
                rwkv_lnx_b)
            p_wkv.append(pw); p_shift.append(ps); s_wkv.append(sw); s_shift.append(ss)
        elif kind == 1:
            out, (p_ret, s_ret) = _ret_layer(h, state_ret, ret_wq, ret_wk, ret_wv, ret_wg, ret_wo)
        else:
            out, fox_outs = _fox_layer(h, cache_fox_k, cache_fox_v, cache_fox_logf, page_table, fox_wq, fox_wk,
                                       fox_wv, fox_wf, fox_bf, fox_wo)
        x, h = _resnorm(x, out, g_mix_post, i, g_ffn_pre, i)
        hm, cs = _ffn_up(h, ffn_wgate, ffn_wup, ffn_conv_w, cb3, conv_state[i], i)
        f = _matmul(hm, ffn_wdown, i, tn=256, kblock=(0, D_FF // 2), name="ffn_down_lo")
        f = _matmul(hm, ffn_wdown, i, tn=256, kblock=(1, D_FF // 2), acc=f, name="ffn_down_hi")
        p_conv.append(cs[:, 0]); s_conv.append(cs[:, 1:].reshape(DEC_B, 2, D_FF))
        if i + 1 < DEPTH and (i + 1) % 3 != 0:
            x, h = _resnorm(x, f, g_ffn_post, i, g_mix_pre, i + 1)
        else:
            x, h = _resnorm(x, f, g_ffn_post, i)
    x3 = x.reshape(NB, TT, D)
    y_prompt = x3[:, N_META:TP]
    y_sample = x3[:, TP:TP + 2 * TS].reshape(DEC_B, TS, D)
    p_fk, p_fv, p_ff, s_fk, s_fv, s_ff = fox_outs
    return (y_prompt, y_sample,
            jnp.stack(p_wkv), jnp.stack(p_shift), p_ret[None],
            p_fk[None], p_fv[None], p_ff[None], jnp.stack(p_conv),
            jnp.stack(s_wkv), jnp.stack(s_shift), s_ret[None],
            s_fk[None], s_fv[None], s_ff[None], jnp.stack(s_conv))
```

```python
import functools

import jax
import jax.numpy as jnp
from jax import lax
from jax.experimental import pallas as pl
from jax.experimental.pallas import tpu as pltpu

F32 = jnp.float32
BF16 = jnp.bfloat16
HIGHEST = lax.Precision.HIGHEST

D = 4096
NB = 4
SEQ = 2048
N_META = 16
TP = SEQ + N_META
DEC_B = 8
TS = 4
TT = 2080
SEQS = ((0, TP), (TP, TS), (TP + TS, TS))
M = NB * TT
DEPTH = 4
RMS_EPS = 1e-6

RWKV_N = 64
RWKV_H = D // RWKV_N
RWKV_GN_EPS = 64e-5
RWKV_L = 64
RWKV_PAIRS = 8
RWKV_ROWS = 1088
RWKV_DECAY_SCALE = 0.6065306597126334
RWKV_LG = 480
RWKV_LG_PAD = 512

RET_H = 16
RET_DK = D // RET_H
RET_DV = 2 * RET_DK
RET_L = 128
ROPE_BASE = 10000.0
PAST_LEN = 8192

FOX_H = 32
FOX_HD = D // FOX_H
FOX_BQ = 512
PAGE = 128
N_PAGES = PAST_LEN // PAGE
DCUM_PAGES = 8
DATT_PAGES = 4

D_FF = 11008
FFN_ROW_BLOCKS = 5

LANES = 128
V7X_VMEM_BYTES = 64 * 1024 * 1024
V7X_VMEM_CAP = 56 * 1024 * 1024

NT_DIMS = (((1,), (1,)), ((), ()))
TN_DIMS = (((0,), (0,)), ((), ()))


def _params(sems, vmem_bytes):
    return pltpu.CompilerParams(
        dimension_semantics=sems,
        vmem_limit_bytes=int(min(max(vmem_bytes * 5 // 4, 16 * 1024 * 1024), V7X_VMEM_CAP)),
    )


def _rms(x, g):
    return x * lax.rsqrt(jnp.mean(x * x, axis=-1, keepdims=True) + RMS_EPS) * g


ROW_TILE = 208


def _mix_body(x_ref, g_ref, mu_ref, sr_ref, w1_ref, w2_ref, w0_ref, a1_ref, a2_ref, a0_ref, g1_ref, g2_ref,
              xr_ref, xk_ref, xv_ref, ld_ref, a_ref, gate_ref, hl_ref, carry, *, tr, nchunks):
    c = pl.program_id(1)

    @pl.when(c == 0)
    def _():
        carry[...] = jnp.zeros_like(carry)

    h = _rms(x_ref[...], g_ref[...])
    row = lax.broadcasted_iota(jnp.int32, (tr, 1), 0) + c * tr
    prev = pltpu.roll(h, 1, 0)
    prev = jnp.where(row == c * tr, carry[7:8, :], prev)
    for s, (start, _) in enumerate(SEQS):
        prev = jnp.where(row == start, sr_ref[0, s:s + 1, :], prev)
    carry[...] = h[tr - 8:, :]
    xx = prev - h
    mix = lambda i: (h + xx * mu_ref[i:i + 1, :]).astype(BF16)
    dot = lambda p, q: jnp.dot(p, q, preferred_element_type=F32)
    xr_ref[...] = mix(0)
    xk_ref[...] = mix(2)
    xv_ref[...] = mix(3)
    z = dot(jnp.tanh(dot(mix(1), w1_ref[...])).astype(BF16), w2_ref[...])
    ld_ref[...] = (-RWKV_DECAY_SCALE) * jax.nn.sigmoid(w0_ref[...] + z)
    z = dot(dot(mix(4), a1_ref[...]).astype(BF16), a2_ref[...])
    a_ref[...] = jax.nn.sigmoid(a0_ref[...] + z).astype(BF16)
    z = dot(jax.nn.sigmoid(dot(mix(5), g1_ref[...])).astype(BF16), g2_ref[...])
    gate_ref[...] = z.astype(BF16)

    @pl.when(c == nchunks - 1)
    def _():
        hl_ref[0] = h[tr - 24:tr - 8, :]


def _rwkv_mix(x, g3, layer, mu, j, shift_rows, w1, w2, w0, a1, a2, a0, g1, g2):
    tr = ROW_TILE
    nchunks = TT // tr
    row_spec = pl.BlockSpec((tr, D), lambda b, c: (b * nchunks + c, 0))
    const = lambda a: pl.BlockSpec((None,) + a.shape[1:], lambda b, c: (j, 0, 0), pipeline_mode=pl.Buffered(1))
    lowrank = (w1, w2, w0, a1, a2, a0, g1, g2)
    bf = jax.ShapeDtypeStruct((M, D), BF16)
    return pl.pallas_call(
        functools.partial(_mix_body, tr=tr, nchunks=nchunks),
        grid=(NB, nchunks),
        in_specs=[
            row_spec,
            pl.BlockSpec((None, 1, D), lambda b, c: (layer, 0, 0)),
            pl.BlockSpec((None, 6, D), lambda b, c: (j, 0, 0)),
            pl.BlockSpec((1, 8, D), lambda b, c: (b, 0, 0)),
        ] + [const(a) for a in lowrank],
        out_specs=[row_spec] * 6 + [pl.BlockSpec((1, 16, D), lambda b, c: (b, 0, 0))],
        out_shape=[bf, bf, bf, jax.ShapeDtypeStruct((M, D), F32), bf, bf, jax.ShapeDtypeStruct((NB, 16, D), F32)],
        scratch_shapes=[pltpu.VMEM((8, D), F32)],
        compiler_params=_params(("arbitrary", "arbitrary"), 48 * 1024 * 1024),
        name="rwkv_mix",
    )(x, g3, mu, shift_rows, *lowrank)


def _resnorm_body(res_ref, y_ref, gp_ref, gn_ref, xo_ref, ho_ref):
    x = res_ref[...] + _rms(y_ref[...], gp_ref[...])
    xo_ref[...] = x
    ho_ref[...] = _rms(x, gn_ref[...]).astype(BF16)


def _res_body(res_ref, y_ref, gp_ref, xo_ref):
    xo_ref[...] = res_ref[...] + _rms(y_ref[...], gp_ref[...])


def _resnorm(res, y, gpost3, lp, gnext3=None, ln=0):
    tr = ROW_TILE
    row_spec = pl.BlockSpec((tr, D), lambda i: (i, 0))
    gspec = lambda l: pl.BlockSpec((None, 1, D), lambda i: (l, 0, 0))
    if gnext3 is None:
        return pl.pallas_call(
            _res_body, grid=(M // tr,), in_specs=[row_spec, row_spec, gspec(lp)], out_specs=row_spec,
            out_shape=jax.ShapeDtypeStruct((M, D), F32),
            compiler_params=_params(("arbitrary",), 32 * 1024 * 1024), name="res_add",
        )(res, y, gpost3), None
    return pl.pallas_call(
        _resnorm_body, grid=(M // tr,), in_specs=[row_spec, row_spec, gspec(lp), gspec(ln)],
        out_specs=[row_spec, row_spec],
        out_shape=[jax.ShapeDtypeStruct((M, D), F32), jax.ShapeDtypeStruct((M, D), BF16)],
        compiler_params=_params(("arbitrary",), 40 * 1024 * 1024), name="res_norm",
    )(res, y, gpost3, gnext3)


def _mm_body(x_ref, w_ref, o_ref):
    o_ref[...] = jnp.dot(x_ref[...], w_ref[...].astype(BF16), preferred_element_type=F32).astype(o_ref.dtype)


def _mm_acc_body(x_ref, w_ref, a_ref, o_ref):
    o_ref[...] = a_ref[...] + jnp.dot(x_ref[...], w_ref[...].astype(BF16), preferred_element_type=F32)


def _matmul(x, w, layer, *, tm=TT // 2, tn=512, out_dtype=F32, name, kblock=None, acc=None):
    m, k = x.shape
    n = w.shape[-1]
    kidx, ksz = (0, k) if kblock is None else kblock
    osz = jnp.dtype(out_dtype).itemsize
    vmem = 2 * tm * ksz * 2 + 2 * ksz * tn * 4 + ksz * tn * 2 + 2 * tm * tn * osz + tm * tn * 4 + (4 << 20)
    in_specs = [
        pl.BlockSpec((tm, ksz), lambda i, j: (i, kidx)),
        pl.BlockSpec((None, ksz, tn), lambda i, j: (layer, kidx, j)),
    ]
    out_spec = pl.BlockSpec((tm, tn), lambda i, j: (i, j))
    args = (x, w)
    body = _mm_body
    aliases = {}
    if acc is not None:
        in_specs.append(out_spec)
        args = (x, w, acc)
        body = _mm_acc_body
        aliases = {2: 0}
        vmem += 2 * tm * tn * 4
    return pl.pallas_call(
        body,
        grid=(m // tm, n // tn),
        in_specs=in_specs,
        out_specs=out_spec,
        out_shape=jax.ShapeDtypeStruct((m, n), out_dtype),
        input_output_aliases=aliases,
        compiler_params=_params(("arbitrary", "arbitrary"), vmem),
        name=name,
    )(*args)


def _ffn_up_body(x_ref, wg_ref, wu_ref, cw_ref, cb_ref, st_ref, h_ref, so_ref):
    wg = wg_ref[...].astype(BF16)
    wu = wu_ref[...].astype(BF16)
    cw = cw_ref[...]
    nblk = FFN_ROW_BLOCKS
    hr = TT // nblk
    gs, us = [], []
    for r in range(nblk):
        x = x_ref[r * hr:(r + 1) * hr, :]
        gs.append(jnp.dot(x, wg, preferred_element_type=F32))
        us.append(jnp.dot(x, wu, preferred_element_type=F32))
    for r in range(nblk):
        g = gs[r]
        base = r * hr
        row = lax.broadcasted_iota(jnp.int32, (hr, 1), 0) + base
        s1 = pltpu.roll(g, 1, 0)
        s2 = pltpu.roll(g, 2, 0)
        if r > 0:
            last = gs[r - 1][hr - 2:hr, :]
            s1 = jnp.where(row == base, last[1:2, :], s1)
            s2 = jnp.where(row == base, last[0:1, :], jnp.where(row == base + 1, last[1:2, :], s2))
        for s, (start, length) in enumerate(SEQS):
            st = st_ref[0, s]
            if base <= start < base + hr:
                s1 = jnp.where(row == start, st[1:2, :], s1)
                s2 = jnp.where(row == start, st[0:1, :], s2)
            if base <= start + 1 < base + hr:
                s2 = jnp.where(row == start + 1, st[1:2, :], s2)
            end = start + length
            if base < end <= base + hr:
                so_ref[0, s] = g[end - 2 - base:end - base, :]
        c = cb_ref[...] + cw[0:1, :] * s2 + cw[1:2, :] * s1 + cw[2:3, :] * g
        act = 0.5 * c * (1.0 + lax.erf(c * (2.0 ** -0.5)))
        h_ref[base:base + hr, :] = (act * us[r]).astype(BF16)


def _ffn_up(h, wg, wu, cw, cb3, state, layer, *, tn=256):
    nj = D_FF // tn
    vmem = TT * D * 2 + 4 * D * tn * 4 + 2 * D * tn * 2 + 2 * TT * tn * 2 + 6 * TT * tn * 4 + (4 << 20)
    return pl.pallas_call(
        _ffn_up_body,
        grid=(NB, nj),
        in_specs=[
            pl.BlockSpec((TT, D), lambda b, j: (b, 0), pipeline_mode=pl.Buffered(1)),
            pl.BlockSpec((None, D, tn), lambda b, j: (layer, 0, j)),
            pl.BlockSpec((None, D, tn), lambda b, j: (layer, 0, j)),
            pl.BlockSpec((None, 3, tn), lambda b, j: (layer, 0, j)),
            pl.BlockSpec((None, 1, tn), lambda b, j: (layer, 0, j)),
            pl.BlockSpec((1, 3, 2, tn), lambda b, j: (b, 0, 0, j)),
        ],
        out_specs=[
            pl.BlockSpec((TT, tn), lambda b, j: (b, j)),
            pl.BlockSpec((1, 3, 2, tn), lambda b, j: (b, 0, 0, j)),
        ],
        out_shape=[jax.ShapeDtypeStruct((M, D_FF), BF16), jax.ShapeDtypeStruct((NB, 3, 2, D_FF), F32)],
        compiler_params=_params(("arbitrary", "arbitrary"), vmem),
        name="ffn_up",
    )(h, wg, wu, cw, cb3, state)


def _rwkv_consts():
    n = 2 * RWKV_L
    lane = lax.broadcasted_iota(jnp.int32, (1, LANES), 1)
    ri = lax.broadcasted_iota(jnp.int32, (n, n), 0)
    ci = lax.broadcasted_iota(jnp.int32, (n, n), 1)
    same = (ri // RWKV_L) == (ci // RWKV_L)
    return dict(
        m0=lane < RWKV_N,
        strict=jnp.logical_and(same, ci < ri),
        incl=jnp.logical_and(same, ci <= ri),
        eye=(ri == ci).astype(F32),
        row=lax.broadcasted_iota(jnp.int32, (RWKV_L, 1), 0),
    )


def _rwkv_prep(r, k, v, ld, a, kk_c, ka_c, cst, active):
    L = RWKV_L
    m0 = cst["m0"]

    def seg(z):
        s0 = jnp.sum(jnp.where(m0, z, 0.0), axis=-1, keepdims=True)
        s1 = jnp.sum(jnp.where(m0, 0.0, z), axis=-1, keepdims=True)
        return jnp.where(m0, s0, s1)

    kraw = k * kk_c
    kk = kraw * lax.rsqrt(seg(kraw * kraw) + 1e-12)
    kmod = k * (1.0 + (a - 1.0) * ka_c)
    aa = -kk
    bb = kk * a
    km = kmod
    vv = v
    if active is not None:
        ld, aa, bb, km, vv = (jnp.where(active, z, 0.0) for z in (ld, aa, bb, km, vv))
    cum = ld
    shift = 1
    while shift < L:
        cum = cum + jnp.where(cst["row"] >= shift, pltpu.roll(cum, shift, 0), 0.0)
        shift *= 2
    cum_l = cum[L - 1:L, :]
    e_neg = jnp.exp(-cum)
    e_rem = jnp.exp(cum_l - cum)

    def stack(z):
        return jnp.concatenate([jnp.where(m0, z, 0.0), jnp.where(m0, 0.0, z)], axis=0)

    rt = stack(r * jnp.exp(cum))
    a2 = stack(aa * jnp.exp(cum - ld)).astype(BF16)
    return dict(
        rt=rt, a2=a2, xa=jnp.concatenate([a2, rt.astype(BF16)], axis=0),
        b2=stack(bb * e_neg).astype(BF16), k2=stack(km * e_neg).astype(BF16),
        bh2=stack(bb * e_rem).astype(BF16), kh2=stack(km * e_rem).astype(BF16),
        v2=stack(vv).astype(BF16), decay=jnp.exp(cum_l), kmod=kmod)


def _rwkv_chunks(preps, ws, cst):
    L = RWKV_L
    n = 2 * L
    dot = lambda p, q: jnp.dot(p, q, preferred_element_type=F32)
    nt = lambda p, q: lax.dot_general(p, q, NT_DIMS, preferred_element_type=F32)
    tn = lambda p, q: lax.dot_general(p, q, TN_DIMS, preferred_element_type=F32)
    xb = [nt(p["xa"], p["b2"]) for p in preps]
    xk = [nt(p["xa"], p["k2"]) for p in preps]
    a_ab = [jnp.where(cst["strict"], z[:n], 0.0) for z in xb]
    a_rb = [jnp.where(cst["incl"], z[n:], 0.0).astype(BF16) for z in xb]
    a_ak = [jnp.where(cst["strict"], z[:n], 0.0).astype(BF16) for z in xk]
    a_rk = [jnp.where(cst["incl"], z[n:], 0.0).astype(BF16) for z in xk]
    akv = [dot(m, p["v2"]) for m, p in zip(a_ak, preps)]
    tinv = [cst["eye"] + z for z in a_ab]
    q = a_ab
    for _ in range(max(L.bit_length() - 2, 0)):
        qb = [z.astype(BF16) for z in q]
        q = [dot(z, z) for z in qb]
        tinv = [t + dot(t.astype(BF16), z.astype(BF16)) for t, z in zip(tinv, q)]
    tu = [dot(t.astype(BF16), jnp.concatenate([u.astype(BF16), p["a2"]], axis=1)).astype(BF16)
          for t, u, p in zip(tinv, akv, preps)]
    zero = jnp.zeros((n, LANES), BF16)
    uv = [jnp.concatenate([z, jnp.concatenate([p["v2"], zero], axis=1)], axis=0) for z, p in zip(tu, preps)]
    ru = [dot(jnp.concatenate([m1, m2], axis=1), z) for m1, m2, z in zip(a_rb, a_rk, uv)]
    bu = [tn(jnp.concatenate([p["bh2"], p["kh2"]], axis=0), z) for p, z in zip(preps, uv)]
    wh = [w.astype(BF16) for w in ws]
    o2 = [dot((p["rt"] + z[:, LANES:]).astype(BF16), w) + z[:, :LANES] for p, z, w in zip(preps, ru, wh)]
    phi = [cst["eye"] * p["decay"] + z[:, LANES:] for p, z in zip(preps, bu)]
    ph = [z.astype(BF16) for z in phi]
    pl_ = [(z - zh.astype(F32)).astype(BF16) for z, zh in zip(phi, ph)]
    wl = [(w - zh.astype(F32)).astype(BF16) for w, zh in zip(ws, wh)]
    hh = [dot(jnp.concatenate([x, y], axis=1), jnp.concatenate([z, z], axis=0)) for x, y, z in zip(ph, pl_, wh)]
    hl = [dot(x, y) for x, y in zip(ph, wl)]
    w_new = [(x + y) + b[:, :LANES] for x, y, b in zip(hh, hl, bu)]
    return [(z[:L] + z[L:], w) for z, w in zip(o2, w_new)]


def _rwkv_body(r_ref, k_ref, v_ref, g_ref, ld_ref, a_ref, kk_ref, ka_ref, rk_ref, lg_ref, lb_ref, s0_ref,
               y_ref, so_ref, w_s, *, npairs):
    L = RWKV_L
    nfull = TP // L
    blk = pl.program_id(2)
    cst = _rwkv_consts()
    m0 = cst["m0"]

    def seg(z):
        s0 = jnp.sum(jnp.where(m0, z, 0.0), axis=-1, keepdims=True)
        s1 = jnp.sum(jnp.where(m0, 0.0, z), axis=-1, keepdims=True)
        return jnp.where(m0, s0, s1)

    def prep_all(rows, active):
        preps = []
        for p in range(npairs):
            ln = slice(LANES * p, LANES * (p + 1))
            preps.append(_rwkv_prep(r_ref[0, rows, ln].astype(F32), k_ref[0, rows, ln].astype(F32),
                                    v_ref[0, rows, ln].astype(F32), ld_ref[0, rows, ln],
                                    a_ref[0, rows, ln].astype(F32), kk_ref[:, ln], ka_ref[:, ln], cst, active))
        return preps

    def finish(rows, preps, outs):
        ys = []
        for p, (prep, (o, _)) in enumerate(zip(preps, outs)):
            ln = slice(LANES * p, LANES * (p + 1))
            mean = seg(o) * (1.0 / RWKV_N)
            xc = o - mean
            var = seg(xc * xc) * (1.0 / RWKV_N)
            on = xc * lax.rsqrt(var + RWKV_GN_EPS) * lg_ref[:, ln] + lb_ref[:, ln]
            bonus = seg(r_ref[0, rows, ln].astype(F32) * prep["kmod"] * rk_ref[:, ln]) * v_ref[0, rows, ln].astype(F32)
            ys.append((on + bonus) * g_ref[0, rows, ln].astype(F32))
        return ys

    @pl.when(blk == 0)
    def _():
        w_s[...] = jnp.zeros_like(w_s)

    def body(c, ws):
        rows = pl.ds(pl.multiple_of(c * L, L), L)
        preps = prep_all(rows, None)
        outs = _rwkv_chunks(preps, ws, cst)
        for p, y in enumerate(finish(rows, preps, outs)):
            y_ref[0, rows, LANES * p:LANES * (p + 1)] = y.astype(BF16)
        return tuple(w for _, w in outs)

    per_blk = RWKV_ROWS // L
    nchunks = jnp.where(blk == 0, per_blk, nfull - per_blk)
    ws = lax.fori_loop(0, nchunks, body, tuple(w_s[p] for p in range(npairs)))
    for p in range(npairs):
        w_s[p] = ws[p]

    @pl.when(blk == 1)
    def _():
        base = RWKV_ROWS
        win = slice(TT - L - base, TT - base)
        lrow = lax.broadcasted_iota(jnp.int32, (L, 1), 0) + (TT - L)
        zero = jnp.zeros((RWKV_N, RWKV_N), F32)
        ytail = [jnp.zeros((L, LANES), F32) for _ in range(npairs)]
        for s, (start, length) in enumerate(SEQS):
            active = jnp.logical_and(lrow >= max(start, nfull * L), lrow < start + length)
            w0s = []
            for p in range(npairs):
                if s == 0:
                    w0s.append(ws[p])
                else:
                    s_in = s0_ref[0, s - 1, 2 * p:2 * p + 2]
                    st = jnp.concatenate([jnp.concatenate([s_in[0], zero], axis=1),
                                          jnp.concatenate([zero, s_in[1]], axis=1)], axis=0)
                    w0s.append(st.T)
            preps = prep_all(win, active)
            outs = _rwkv_chunks(preps, w0s, cst)
            for p, (y, (_, w_new)) in enumerate(zip(finish(win, preps, outs), outs)):
                ytail[p] = jnp.where(active, y, ytail[p])
                wt = w_new.T
                so_ref[0, s, 2 * p] = wt[:RWKV_N, :RWKV_N]
                so_ref[0, s, 2 * p + 1] = wt[RWKV_N:, RWKV_N:]
        for p in range(npairs):
            y_ref[0, nfull * L - base:TT - base, LANES * p:LANES * (p + 1)] = \
                ytail[p][nfull * L - (TT - L):].astype(BF16)


def _rwkv(r, k, v, g, ld, a, kk3, ka3, rk3, lg3, lb3, s0, j, *, npairs=RWKV_PAIRS):
    pw = LANES * npairs
    nh = 2 * npairs
    act = lambda: pl.BlockSpec((1, RWKV_ROWS, pw), lambda b, p, t: (b, t, p))
    par = lambda: pl.BlockSpec((None, 1, pw), lambda b, p, t: (j, 0, p))
    vmem = 2 * RWKV_ROWS * pw * (5 * 2 + 4) + 2 * RWKV_ROWS * pw * 2 + (20 << 20)
    return pl.pallas_call(
        functools.partial(_rwkv_body, npairs=npairs),
        grid=(NB, RWKV_H // nh, 2),
        in_specs=[act(), act(), act(), act(), act(), act(), par(), par(), par(), par(), par(),
                  pl.BlockSpec((None, 1, 2, nh, RWKV_N, RWKV_N), lambda b, p, t: (j, b, 0, p, 0, 0))],
        out_specs=[act(), pl.BlockSpec((1, 3, nh, RWKV_N, RWKV_N), lambda b, p, t: (b, 0, p, 0, 0))],
        out_shape=[jax.ShapeDtypeStruct((NB, TT, D), BF16),
                   jax.ShapeDtypeStruct((NB, 3, RWKV_H, RWKV_N, RWKV_N), F32)],
        scratch_shapes=[pltpu.VMEM((npairs, LANES, LANES), F32)],
        compiler_params=_params(("arbitrary", "arbitrary", "arbitrary"), vmem),
        name="rwkv7",
    )(r, k, v, g, ld, a, kk3, ka3, rk3, lg3, lb3, s0)


def _ret_body(q_ref, k_ref, v_ref, g_ref, cos_ref, sin_ref, lg_ref, s0_ref, y_ref, so_ref):
    L = RET_L
    nfull = TP // L
    half = RET_DK // 2
    lg = lg_ref[:, :1]
    ii = lax.broadcasted_iota(jnp.int32, (L, L), 0)
    jj = lax.broadcasted_iota(jnp.int32, (L, L), 1)
    ri = lax.broadcasted_iota(jnp.int32, (L, 1), 0)
    dfull = jnp.where(jj <= ii, jnp.exp((ii - jj).astype(F32) * lg), 0.0)

    def rot(z, cs, sn):
        z1 = z[:, :half]
        z2 = z[:, half:]
        return jnp.concatenate([z1 * cs - z2 * sn, z1 * sn + z2 * cs], axis=1)

    def chunk(rows, s_in, off, length):
        cs = cos_ref[rows, :]
        sn = sin_ref[rows, :]
        q = rot(q_ref[rows, :], cs, sn).astype(BF16)
        k = rot(k_ref[rows, :], cs, sn) * (RET_DK ** -0.5)
        v = v_ref[rows, :]
        nloc = (ri - off).astype(F32)
        act = jnp.logical_and(ri >= off, ri < off + length)
        if off == 0 and length == L:
            dmat = dfull
        else:
            dmat = jnp.where(jnp.logical_and(jj >= off, ii < off + length), dfull, 0.0)
        inner = jnp.exp((nloc + 1.0) * lg)
        kdec = jnp.where(act, jnp.exp((length - 1.0 - nloc) * lg), 0.0)
        sc = lax.dot_general(q, k.astype(BF16), NT_DIMS, preferred_element_type=F32) * dmat
        o = jnp.dot(sc.astype(BF16), v, preferred_element_type=F32)
        o = o + jnp.dot(q, s_in.astype(BF16), preferred_element_type=F32) * inner
        s_out = jnp.exp(length * lg) * s_in + lax.dot_general((k * kdec).astype(BF16), v, TN_DIMS,
                                                                preferred_element_type=F32)
        o = o * lax.rsqrt(jnp.mean(o * o, axis=-1, keepdims=True) + RMS_EPS)
        gate = g_ref[rows, :].astype(F32)
        return gate * jax.nn.sigmoid(gate) * o, s_out, act

    so_ref[0, 0, 0] = jnp.zeros((RET_DK, RET_DV), F32)

    def body(c, carry):
        rows = pl.ds(pl.multiple_of(c * L, L), L)
        y, s_out, _ = chunk(rows, so_ref[0, 0, 0], 0, L)
        y_ref[rows, :] = y.astype(BF16)
        so_ref[0, 0, 0] = s_out
        return carry

    lax.fori_loop(0, nfull, body, 0)

    win = slice(TT - L, TT)
    ytail = jnp.zeros((L, RET_DV), F32)
    for s, (start, length) in enumerate(SEQS):
        lo = max(start, nfull * L)
        s_in = so_ref[0, 0, 0] if s == 0 else s0_ref[0, s - 1, 0]
        y, s_out, act = chunk(win, s_in, lo - (TT - L), start + length - lo)
        ytail = jnp.where(act, y, ytail)
        so_ref[0, s, 0] = s_out
    y_ref[nfull * L:TT, :] = ytail[nfull * L - (TT - L):].astype(BF16)


def _retention(q, k, v, g, cos, sin, lg3, s0):
    spec = lambda w: pl.BlockSpec((TT, w), lambda b, h: (b, h))
    tab = pl.BlockSpec((TT, RET_DK // 2), lambda b, h: (0, 0))
    vmem = 2 * TT * (2 * RET_DK * 4 + 2 * RET_DV * 2 + RET_DV * 2 + 2 * 128 * 4) + 12 * RET_DK * RET_DV * 4 + (8 << 20)
    return pl.pallas_call(
        _ret_body,
        grid=(NB, RET_H),
        in_specs=[spec(RET_DK), spec(RET_DK), spec(RET_DV), spec(RET_DV), tab, tab,
                  pl.BlockSpec((None, 1, LANES), lambda b, h: (h, 0, 0)),
                  pl.BlockSpec((1, 2, 1, RET_DK, RET_DV), lambda b, h: (b, 0, h, 0, 0))],
        out_specs=[spec(RET_DV), pl.BlockSpec((1, 3, 1, RET_DK, RET_DV), lambda b, h: (b, 0, h, 0, 0))],
        out_shape=[jax.ShapeDtypeStruct((M, RET_H * RET_DV), BF16),
                   jax.ShapeDtypeStruct((NB, 3, RET_H, RET_DK, RET_DV), F32)],
        compiler_params=_params(("arbitrary", "arbitrary"), vmem),
        name="retention",
    )(q, k, v, g, cos, sin, lg3, s0)


def _fproj_body(x_ref, w_ref, b_ref, o_ref):
    z = jnp.dot(x_ref[...], w_ref[...].astype(BF16), preferred_element_type=F32) + b_ref[...]
    o_ref[...] = jax.nn.log_sigmoid(z)


def _fox_logf(h, wf_pad, bf_pad, *, tm=416):
    return pl.pallas_call(
        _fproj_body, grid=(M // tm,),
        in_specs=[pl.BlockSpec((tm, D), lambda i: (i, 0)), pl.BlockSpec((D, LANES), lambda i: (0, 0)),
                  pl.BlockSpec((1, LANES), lambda i: (0, 0))],
        out_specs=pl.BlockSpec((tm, LANES), lambda i: (i, 0)),
        out_shape=jax.ShapeDtypeStruct((M, LANES), F32),
        compiler_params=_params(("arbitrary",), 24 * 1024 * 1024), name="fox_logf",
    )(h, wf_pad, bf_pad)


def _tri(n):
    return (lax.broadcasted_iota(jnp.int32, (n, n), 1) <= lax.broadcasted_iota(jnp.int32, (n, n), 0)).astype(F32)


def _pcum_body(f_ref, c_ref):
    n = LANES
    tri = _tri(n)
    carry = jnp.zeros((1, LANES), F32)
    nfull = TP // n
    for blk in range(nfull):
        cb = jnp.dot(tri, f_ref[0, blk * n:(blk + 1) * n, :], precision=HIGHEST, preferred_element_type=F32) + carry
        c_ref[0, blk * n:(blk + 1) * n, :] = cb
        carry = cb[n - 1:n, :]
    row = lax.broadcasted_iota(jnp.int32, (n, 1), 0) + (TT - n)
    x = jnp.where(row >= nfull * n, f_ref[0, TT - n:TT, :], 0.0)
    cb = jnp.dot(tri, x, precision=HIGHEST, preferred_element_type=F32) + carry
    c_ref[0, nfull * n:TT, :] = cb[nfull * n - (TT - n):, :]


def _fox_prompt_cum(f3):
    spec = pl.BlockSpec((1, TT, LANES), lambda b: (b, 0, 0))
    return pl.pallas_call(
        _pcum_body, grid=(NB,), in_specs=[spec], out_specs=spec,
        out_shape=jax.ShapeDtypeStruct((NB, TT, LANES), F32),
        compiler_params=_params(("arbitrary",), 16 * 1024 * 1024), name="fox_cum",
    )(f3)


def _softmax_step(s, vb, m, l, acc):
    m_new = jnp.maximum(m, jnp.max(s, axis=-1, keepdims=True))
    alpha = jnp.exp(m - m_new)
    p = jnp.exp(s - m_new)
    l = alpha * l + jnp.sum(p, axis=-1, keepdims=True)
    acc = alpha * acc + jnp.dot(p.astype(BF16), vb, preferred_element_type=F32)
    return m_new, l, acc


def _fox_body(q_ref, k_ref, v_ref, c_ref, ct_ref, dec_ref, y_ref):
    h = pl.program_id(1)
    bq = FOX_BQ
    nq = TP // bq
    scale = FOX_HD ** -0.5
    pick = (lax.broadcasted_iota(jnp.int32, (LANES, LANES), 0) == h).astype(F32)

    def attend(qrows, nrows, kv_blocks, finish):
        q = (q_ref[qrows, :] * scale).astype(BF16)
        cq = jnp.dot(c_ref[0, qrows, :], pick, precision=HIGHEST, preferred_element_type=F32)[:, :1]

        def scores(krows, mask):
            s = lax.dot_general(q, k_ref[krows, :].astype(BF16), NT_DIMS, preferred_element_type=F32)
            s = s + cq - ct_ref[:, krows]
            return s if mask is None else jnp.where(mask, s, -jnp.inf)

        m = jnp.full((nrows, 1), -jnp.inf, F32)
        l = jnp.zeros((nrows, 1), F32)
        acc = jnp.zeros((nrows, FOX_HD), F32)
        s_next = scores(*kv_blocks[0])
        for idx, (krows, _) in enumerate(kv_blocks):
            s = s_next
            if idx + 1 < len(kv_blocks):
                s_next = scores(*kv_blocks[idx + 1])
            m, l, acc = _softmax_step(s, v_ref[krows, :].astype(BF16), m, l, acc)
            yield
        finish(acc / l)

    rr = lax.broadcasted_iota(jnp.int32, (bq, bq), 0)
    cc = lax.broadcasted_iota(jnp.int32, (bq, bq), 1)
    causal = cc <= rr
    blocks = [slice(i * bq, (i + 1) * bq) for i in range(nq)]

    def store(rows):
        def finish(o):
            y_ref[rows, :] = o.astype(BF16)
        return finish

    streams = [attend(blocks[qi], bq, [(blocks[kj], causal if kj == qi else None) for kj in range(qi + 1)],
                      store(blocks[qi])) for qi in range(nq)]
    nt = TT - nq * bq
    npr = TP - nq * bq
    qrow = lax.broadcasted_iota(jnp.int32, (nt, LANES), 0) + nq * bq
    kcol = lax.broadcasted_iota(jnp.int32, (nt, LANES), 1) + (TT - LANES)
    tail_mask = jnp.logical_and(kcol >= nq * bq, kcol <= qrow)

    def finish_tail(o):
        y_ref[nq * bq:TT, :] = jnp.concatenate(
            [o[:npr], dec_ref[0], jnp.zeros((TT - TP - 2 * TS, FOX_HD), F32)], axis=0).astype(BF16)

    streams.append(attend(slice(nq * bq, TT), nt, [(b, None) for b in blocks] + [(slice(TT - LANES, TT), tail_mask)],
                          finish_tail))
    while streams:
        streams = [g for g in streams if next(g, True) is None]


def _fox_prompt(q, k, v, c3, ct3, dec):
    spec = pl.BlockSpec((TT, FOX_HD), lambda b, h: (b, h))
    return pl.pallas_call(
        _fox_body,
        grid=(NB, FOX_H),
        in_specs=[spec, spec, spec,
                  pl.BlockSpec((1, TT, LANES), lambda b, h: (b, 0, 0)),
                  pl.BlockSpec((None, None, 1, TT), lambda b, h: (b, h, 0, 0)),
                  pl.BlockSpec((1, 2 * TS, FOX_HD), lambda b, h: (b, 0, h))],
        out_specs=spec,
        out_shape=jax.ShapeDtypeStruct((M, D), BF16),
        compiler_params=_params(("arbitrary", "arbitrary"), 40 * 1024 * 1024),
        name="fox_prompt",
    )(q, k, v, c3, ct3, dec)


def _dcum_body(pt_ref, *refs):
    del pt_ref
    page_refs = refs[:DCUM_PAGES]
    nf_ref, o_ref, carry = refs[DCUM_PAGES:]
    p = pl.program_id(1)
    tri = _tri(PAGE)

    @pl.when(p == 0)
    def _():
        carry[...] = jnp.zeros_like(carry)

    @pl.when(p < N_PAGES // DCUM_PAGES)
    def _():
        c = carry[...]
        for i, ref in enumerate(page_refs):
            cb = jnp.dot(tri, ref[0], precision=HIGHEST, preferred_element_type=F32) + c
            o_ref[0, i] = cb
            c = cb[PAGE - 1:PAGE, :]
        carry[...] = c

    @pl.when(p == N_PAGES // DCUM_PAGES)
    def _():
        cb = jnp.dot(tri, nf_ref[0], precision=HIGHEST, preferred_element_type=F32) + carry[...]
        for i in range(DCUM_PAGES):
            o_ref[0, i] = cb


def _fox_decode_cum(page_table, cache_f, new_f):
    nsteps = N_PAGES // DCUM_PAGES

    def page_spec(i):
        return pl.BlockSpec((1, PAGE, FOX_H),
                            lambda b, p, pt: (pt[b, jnp.minimum(p, nsteps - 1) * DCUM_PAGES + i], 0, 0))

    grid_spec = pltpu.PrefetchScalarGridSpec(
        num_scalar_prefetch=1,
        grid=(DEC_B, nsteps + 1),
        in_specs=[page_spec(i) for i in range(DCUM_PAGES)]
        + [pl.BlockSpec((1, PAGE, FOX_H), lambda b, p, pt: (b, 0, 0))],
        out_specs=pl.BlockSpec((1, DCUM_PAGES, PAGE, FOX_H), lambda b, p, pt: (b, p, 0, 0)),
        scratch_shapes=[pltpu.VMEM((1, FOX_H), F32)],
    )
    return pl.pallas_call(
        _dcum_body, grid_spec=grid_spec,
        out_shape=jax.ShapeDtypeStruct((DEC_B, N_PAGES + DCUM_PAGES, PAGE, FOX_H), F32),
        compiler_params=_params(("arbitrary", "arbitrary"), 16 * 1024 * 1024), name="fox_decode_cum",
    )(page_table, *([cache_f] * DCUM_PAGES), new_f)


def _dattn_body(pt_ref, q_ref, *refs):
    del pt_ref
    k_refs = refs[:DATT_PAGES]
    v_refs = refs[DATT_PAGES:2 * DATT_PAGES]
    crow_ref, cq_ref, kn_ref, vn_ref, cn_ref, o_ref, m_s, l_s, acc_s = refs[2 * DATT_PAGES:]
    p = pl.program_id(1)
    nrow = TS * FOX_H
    nsteps = N_PAGES // DATT_PAGES

    @pl.when(p == 0)
    def _():
        m_s[...] = jnp.full_like(m_s, -jnp.inf)
        l_s[...] = jnp.zeros_like(l_s)
        acc_s[...] = jnp.zeros_like(acc_s)

    q = (q_ref[0] * (FOX_HD ** -0.5)).astype(BF16)
    cq = cq_ref[0][:, :1]
    hrow = lax.broadcasted_iota(jnp.int32, (nrow, 1), 0) % FOX_H

    def update(ss, vbs):
        m_prev = m_s[...]
        m_new = m_prev
        for s in ss:
            m_new = jnp.maximum(m_new, jnp.max(s, axis=-1, keepdims=True))
        alpha = jnp.exp(m_prev - m_new)
        l = alpha * l_s[...]
        acc = alpha * acc_s[...]
        for s, vb in zip(ss, vbs):
            pr = jnp.exp(s - m_new)
            l = l + jnp.sum(pr, axis=-1, keepdims=True)
            acc = acc + jnp.dot(pr.astype(BF16), vb, preferred_element_type=F32)
        m_s[...] = m_new
        l_s[...] = l
        acc_s[...] = acc

    @pl.when(p < nsteps)
    def _():
        hcol = lax.broadcasted_iota(jnp.int32, (1, PAGE * FOX_H), 1) % FOX_H
        same = hcol == hrow
        ss = []
        for i, k_ref in enumerate(k_refs):
            kb = k_ref[0].reshape(PAGE * FOX_H, FOX_HD).astype(BF16)
            s = lax.dot_general(q, kb, NT_DIMS, preferred_element_type=F32) + cq - crow_ref[0, i]
            ss.append(jnp.where(same, s, -jnp.inf))
        update(ss, [v_ref[0].reshape(PAGE * FOX_H, FOX_HD).astype(BF16) for v_ref in v_refs])

    @pl.when(p == nsteps)
    def _():
        s = lax.dot_general(q, kn_ref[0].astype(BF16), NT_DIMS, preferred_element_type=F32) + cq - cn_ref[0]
        col = lax.broadcasted_iota(jnp.int32, (1, nrow), 1)
        row = lax.broadcasted_iota(jnp.int32, (nrow, 1), 0)
        valid = jnp.logical_and(col % FOX_H == hrow, col // FOX_H <= row // FOX_H)
        update([jnp.where(valid, s, -jnp.inf)], [vn_ref[0].astype(BF16)])
        o_ref[0] = acc_s[...] / l_s[...]


def _fox_decode_attn(page_table, q2, cache_k, cache_v, crow, cq, kn, vn, cn):
    nrow = TS * FOX_H
    nsteps = N_PAGES // DATT_PAGES

    def page_spec(i):
        return pl.BlockSpec((1, PAGE, FOX_H, FOX_HD),
                            lambda b, p, pt: (pt[b, jnp.minimum(p, nsteps - 1) * DATT_PAGES + i], 0, 0, 0))

    per_b = pl.BlockSpec((1, nrow, FOX_HD), lambda b, p, pt: (b, 0, 0))
    grid_spec = pltpu.PrefetchScalarGridSpec(
        num_scalar_prefetch=1,
        grid=(DEC_B, nsteps + 1),
        in_specs=[per_b]
        + [page_spec(i) for i in range(DATT_PAGES)] * 2
        + [pl.BlockSpec((1, DATT_PAGES, 1, PAGE * FOX_H), lambda b, p, pt: (b, jnp.minimum(p, nsteps - 1), 0, 0)),
           per_b, per_b, per_b,
           pl.BlockSpec((1, 1, nrow), lambda b, p, pt: (b, 0, 0))],
        out_specs=per_b,
        scratch_shapes=[pltpu.VMEM((nrow, 1), F32), pltpu.VMEM((nrow, 1), F32), pltpu.VMEM((nrow, FOX_HD), F32)],
    )
    return pl.pallas_call(
        _dattn_body, grid_spec=grid_spec,
        out_shape=jax.ShapeDtypeStruct((DEC_B, nrow, FOX_HD), F32),
        compiler_params=_params(("arbitrary", "arbitrary"), 44 * 1024 * 1024), name="fox_decode_attn",
    )(page_table, q2, *([cache_k] * DATT_PAGES), *([cache_v] * DATT_PAGES), crow, cq, kn, vn, cn)


def _sample_rows(a):
    n = a.shape[-1]
    return a.reshape(NB, TT, n)[:, TP:TP + 2 * TS].reshape(DEC_B, TS, n)


def _prompt_rows(a):
    n = a.shape[-1]
    return a.reshape(NB, TT, n)[:, :TP]


def _pack_states(prompt_like_zero, sample):
    s = sample.reshape((NB, 2) + sample.shape[1:])
    return jnp.concatenate([jnp.zeros_like(s[:, :1]) if prompt_like_zero else s[:, :0], s], axis=1)


def _rwkv_layer(x, layer, j, gpre3, state_wkv6, state_shift, mu, w0, w1, w2, a0, a1, a2, g1p, g2p, k_k, k_a, r_k,
                wr, wk, wv, wo, lnx_g, lnx_b):
    shift_rows = jnp.concatenate(
        [jnp.zeros((NB, 1, D), F32), state_shift[j].reshape(NB, 2, D), jnp.zeros((NB, 5, D), F32)], axis=1)
    row3 = lambda p: p.reshape(p.shape[0], 1, D)
    bf = lambda p: p.astype(BF16)
    xr, xk, xv, ld, a, g, hl = _rwkv_mix(x, gpre3, layer, mu, j, shift_rows, bf(w1), bf(w2), row3(w0), bf(a1), bf(a2),
                                         row3(a0), bf(g1p), bf(g2p))
    mm = functools.partial(_matmul, out_dtype=BF16)
    r = mm(xr, wr, j, name="rwkv_r")
    k = mm(xk, wk, j, name="rwkv_k")
    v = mm(xv, wv, j, name="rwkv_v")
    t3 = lambda z: z.reshape(NB, TT, D)
    y, states = _rwkv(t3(r), t3(k), t3(v), t3(g), t3(ld), t3(a), row3(k_k), row3(k_a), r_k.reshape(-1, 1, D),
                      row3(lnx_g), row3(lnx_b), state_wkv6, j)
    out = _matmul(y.reshape(M, D), wo, j, name="rwkv_o")
    p_wkv = states[:, 0]
    s_wkv = states[:, 1:].reshape(DEC_B, RWKV_H, RWKV_N, RWKV_N)
    p_shift = hl[:, 7]
    s_shift = jnp.stack([hl[:, 11], hl[:, 15]], axis=1).reshape(DEC_B, D)
    return out, (p_wkv, p_shift, s_wkv, s_shift)


def _ret_tables():
    half = RET_DK // 2
    inv = ROPE_BASE ** (-jnp.arange(half, dtype=F32) / half)
    row = jnp.arange(TT)
    pos = jnp.where(row < TP, row, PAST_LEN + jnp.maximum(row - TP, 0) % TS)
    ang = pos.astype(F32)[:, None] * inv[None, :]
    lg = jnp.log1p(-jnp.exp2(-5.0 - jnp.arange(RET_H, dtype=F32)))
    return jnp.cos(ang), jnp.sin(ang), jnp.broadcast_to(lg[:, None, None], (RET_H, 1, LANES))


def _ret_layer(h, state_ret, wq, wk, wv, wg, wo):
    q = _matmul(h, wq, 0, name="ret_q")
    k = _matmul(h, wk, 0, name="ret_k")
    v = _matmul(h, wv, 0, out_dtype=BF16, name="ret_v")
    g = _matmul(h, wg, 0, out_dtype=BF16, name="ret_g")
    cos, sin, lg3 = _ret_tables()
    s0 = state_ret[0].reshape(NB, 2, RET_H, RET_DK, RET_DV)
    y, states = _retention(q, k, v, g, cos, sin, lg3, s0)
    out = _matmul(y, wo, 0, kblock=(0, D), name="ret_o_lo")
    out = _matmul(y, wo, 0, kblock=(1, D), acc=out, name="ret_o_hi")
    return out, (states[:, 0], states[:, 1:].reshape(DEC_B, RET_H, RET_DK, RET_DV))


def _fox_layer(h, cache_k, cache_v, cache_f, page_table, wq, wk, wv, wf, bf, wo):
    q = _matmul(h, wq, 0, name="fox_q")
    k = _matmul(h, wk, 0, name="fox_k")
    v = _matmul(h, wv, 0, name="fox_v")
    wf_pad = jnp.pad(wf[0], ((0, 0), (0, LANES - FOX_H)))
    bf_pad = jnp.pad(bf[0], (0, LANES - FOX_H)).reshape(1, LANES)
    f = _fox_logf(h, wf_pad, bf_pad)
    f3 = f.reshape(NB, TT, LANES)
    c3 = _fox_prompt_cum(f3)
    ct3 = jnp.swapaxes(c3[:, :, :FOX_H], 1, 2).reshape(NB, FOX_H, 1, TT)
    nrow = TS * FOX_H
    qs, ks, vs = (_sample_rows(z) for z in (q, k, v))
    fs = _sample_rows(f)[:, :, :FOX_H]
    new_f = jnp.pad(fs, ((0, 0), (0, PAGE - TS), (0, 0)))
    call = _fox_decode_cum(page_table, cache_f[0], new_f)
    crow = call[:, :N_PAGES].reshape(DEC_B, N_PAGES, 1, PAGE * FOX_H)
    cnew = call[:, N_PAGES, :TS]
    cq = jnp.broadcast_to(cnew.reshape(DEC_B, nrow, 1), (DEC_B, nrow, FOX_HD))
    o2 = _fox_decode_attn(page_table, qs.reshape(DEC_B, nrow, FOX_HD), cache_k[0], cache_v[0], crow, cq,
                          ks.reshape(DEC_B, nrow, FOX_HD), vs.reshape(DEC_B, nrow, FOX_HD),
                          cnew.reshape(DEC_B, 1, nrow))
    dec = o2.reshape(NB, 2 * TS, D)
    y = _fox_prompt(q, k, v, c3, ct3, dec)
    out = _matmul(y, wo, 0, name="fox_o")
    heads = lambda z, t: z.reshape(-1, t, FOX_H, FOX_HD)
    outs = (heads(_prompt_rows(k), TP), heads(_prompt_rows(v), TP), _prompt_rows(f)[:, :, :FOX_H],
            heads(ks, TS), heads(vs, TS), fs)
    return out, outs


def kernel(x_prompt, x_sample, state_rwkv_wkv, state_rwkv_shift, state_ret, cache_fox_k, cache_fox_v, cache_fox_logf, state_ffn_conv, page_table, meta_tokens, norm_mix_pre, norm_mix_post, norm_ffn_pre, norm_ffn_post, rwkv_mu, rwkv_w0, rwkv_w1, rwkv_w2, rwkv_a0, rwkv_a1, rwkv_a2, rwkv_g1, rwkv_g2, rwkv_k_k, rwkv_k_a, rwkv_r_k, rwkv_wr, rwkv_wk, rwkv_wv, rwkv_wo, rwkv_lnx_g, rwkv_lnx_b, ret_wq, ret_wk, ret_wv, ret_wg, ret_wo, fox_wq, fox_wk, fox_wv, fox_wf, fox_bf, fox_wo, ffn_wgate, ffn_wup, ffn_conv_w, ffn_conv_b, ffn_wdown):
    meta = jnp.broadcast_to(meta_tokens[None].astype(F32), (NB, N_META, D))
    x = jnp.concatenate(
        [meta, x_prompt, x_sample.reshape(NB, 2 * TS, D), jnp.zeros((NB, TT - TP - 2 * TS, D), F32)], axis=1
    ).reshape(M, D)
    row3 = lambda p: p.reshape(p.shape[0], 1, p.shape[-1])
    g_mix_pre, g_mix_post, g_ffn_pre, g_ffn_post = (row3(p) for p in (norm_mix_pre, norm_mix_post, norm_ffn_pre, norm_ffn_post))
    g1p = jnp.pad(rwkv_g1, ((0, 0), (0, 0), (0, RWKV_LG_PAD - RWKV_LG)))
    g2p = jnp.pad(rwkv_g2, ((0, 0), (0, RWKV_LG_PAD - RWKV_LG), (0, 0)))
    wkv6 = state_rwkv_wkv.reshape(-1, NB, 2, RWKV_H, RWKV_N, RWKV_N)
    conv_state = jnp.concatenate(
        [jnp.zeros((DEPTH, NB, 1, 2, D_FF), F32), state_ffn_conv.reshape(DEPTH, NB, 2, 2, D_FF)], axis=2)
    cb3 = row3(ffn_conv_b)

    p_wkv, p_shift, s_wkv, s_shift, p_conv, s_conv = [], [], [], [], [], []
    h = None
    for i in range(DEPTH):
        kind = i % 3
        j = i // 3
        if kind == 0:
            out, (pw, ps, sw, ss) = _rwkv_layer(
                x, i, j, g_mix_pre, wkv6, state_rwkv_shift, rwkv_mu, rwkv_w0, rwkv_w1, rwkv_w2, rwkv_a0, rwkv_a1,
                rwkv_a2, g1p, g2p, rwkv_k_k, rwkv_k_a, rwkv_r_k, rwkv_wr, rwkv_wk, rwkv_wv, rwkv_wo, rwkv_lnx_g,
                rwkv_lnx_b)
            p_wkv.append(pw); p_shift.append(ps); s_wkv.append(sw); s_shift.append(ss)
        elif kind == 1:
            out, (p_ret, s_ret) = _ret_layer(h, state_ret, ret_wq, ret_wk, ret_wv, ret_wg, ret_wo)
        else:
            out, fox_outs = _fox_layer(h, cache_fox_k, cache_fox_v, cache_fox_logf, page_table, fox_wq, fox_wk,
                                       fox_wv, fox_wf, fox_bf, fox_wo)
        x, h = _resnorm(x, out, g_mix_post, i, g_ffn_pre, i)
        hm, cs = _ffn_up(h, ffn_wgate, ffn_wup, ffn_conv_w, cb3, conv_state[i], i)
        f = _matmul(hm, ffn_wdown, i, tn=256, kblock=(0, D_FF // 2), name="ffn_down_lo")
        f = _matmul(hm, ffn_wdown, i, tn=256, kblock=(1, D_FF // 2), acc=f, name="ffn_down_hi")
        p_conv.append(cs[:, 0]); s_conv.append(cs[:, 1:].reshape(DEC_B, 2, D_FF))
        if i + 1 < DEPTH and (i + 1) % 3 != 0:
            x, h = _resnorm(x, f, g_ffn_post, i, g_mix_pre, i + 1)
        else:
            x, h = _resnorm(x, f, g_ffn_post, i)
    x3 = x.reshape(NB, TT, D)
    y_prompt = x3[:, N_META:TP]
    y_sample = x3[:, TP:TP + 2 * TS].reshape(DEC_B, TS, D)
    p_fk, p_fv, p_ff, s_fk, s_fv, s_ff = fox_outs
    return (y_prompt, y_sample,
            jnp.stack(p_wkv), jnp.stack(p_shift), p_ret[None],
            p_fk[None], p_fv[None], p_ff[None], jnp.stack(p_conv),
            jnp.stack(s_wkv), jnp.stack(s_shift), s_ret[None],
            s_fk[None], s_fv[None], s_ff[None], jnp.stack(s_conv))
```
